```python
import jax, jax.numpy as jnp
from jax import lax
import numpy as np

D_MODEL = 1024
BATCH = 2
SEQ = 8192
DEPTH = 2
DEC_BATCH = 16
DEC_SEQ = 2048
PAST_LEN = 128

HEAD_DIM = 64
GRID_W = 64
EPS = 1e-6
A_HEADS = 6
A_PATTERNS = ((128, 1), (512, 4), (2048, 16))
A_BLOCK = 64
ROPE_THETA = 500000.0
ROPE_DIMS = HEAD_DIM // 4
B_HEADS = 4
B_KV_HEADS = 2
B_QBLOCK = 128
AXIAL_THETA = 10000.0
C_HEADS = 6
NA_ROWS = 8
NA_COLS = 16

A_W = A_HEADS * HEAD_DIM
B_W = B_HEADS * HEAD_DIM
B_KV_W = B_KV_HEADS * HEAD_DIM
C_W = C_HEADS * HEAD_DIM
MIX_W = A_W + B_W + C_W
IN_W = 4 * A_W + 2 * B_W + 2 * B_KV_W + 4 * C_W

kernel_name = "hybrid_dilated_gqa_neighbourhood_encoder"


def _rmsnorm(x, g):
    xf = x.astype(jnp.float32)
    r = lax.rsqrt(jnp.mean(xf * xf, axis=-1, keepdims=True) + EPS)
    return (xf * r * g.astype(jnp.float32)).astype(x.dtype)


def _freqs(n_dims, theta):
    return theta ** (-jnp.arange(0, n_dims, 2, dtype=jnp.float32) / n_dims)


def _rotate(x, ang):
    n = ang.shape[-1]
    cos = jnp.cos(ang)[None, :, None, :].astype(x.dtype)
    sin = jnp.sin(ang)[None, :, None, :].astype(x.dtype)
    x1, x2 = x[..., :n], x[..., n:]
    return jnp.concatenate([x1 * cos - x2 * sin, x2 * cos + x1 * sin], axis=-1)


def _window_attn(q, k, v, radius):
    N, L, H, dh = q.shape
    nb = -(-L // A_BLOCK)
    Lp = nb * A_BLOCK
    nk = A_BLOCK + 2 * radius
    qb = jnp.pad(q, ((0, 0), (0, Lp - L), (0, 0), (0, 0))).reshape(N, nb, A_BLOCK, H, dh)
    pad = ((0, 0), (radius, Lp - L + radius), (0, 0), (0, 0))
    kidx = (jnp.arange(nb) * A_BLOCK)[:, None] + jnp.arange(nk)[None, :]
    kb = jnp.pad(k, pad)[:, kidx]
    vb = jnp.pad(v, pad)[:, kidx]
    qpos = (jnp.arange(nb) * A_BLOCK)[:, None] + jnp.arange(A_BLOCK)[None, :]
    kpos = (kidx - radius)[:, None, :]
    valid = (jnp.abs(kpos - qpos[:, :, None]) <= radius) & (kpos >= 0) & (kpos < L)
    s = jnp.einsum('nbqhd,nbkhd->nbhqk', qb, kb).astype(jnp.float32) * (dh ** -0.5)
    s = jnp.where(valid[None, :, None], s, -jnp.inf)
    m = jnp.max(s, axis=-1, keepdims=True)
    p = jnp.exp(s - m)
    l = jnp.sum(p, axis=-1, keepdims=True)
    o = jnp.einsum('nbhqk,nbkhd->nbqhd', (p / l).astype(v.dtype), vb)
    lse = (m + jnp.log(l))[..., 0]
    o = o.reshape(N, Lp, H, dh)[:, :L]
    lse = lse.transpose(0, 1, 3, 2).reshape(N, Lp, H)[:, :L]
    return o, lse


def _to_residues(t, dil):
    B, S, H, dh = t.shape
    return t.reshape(B, S // dil, dil, H, dh).transpose(0, 2, 1, 3, 4).reshape(B * dil, S // dil, H, dh)


def _from_residues(t, B, dil):
    N, L = t.shape[0], t.shape[1]
    rest = t.shape[2:]
    t = t.reshape((B, dil, L) + rest)
    t = jnp.swapaxes(t, 1, 2)
    return t.reshape((B, L * dil) + rest)


def _dilated_attention(q, k, v):
    B, S, H, dh = q.shape
    outs, lses = [], []
    for window, dil in A_PATTERNS:
        radius = window // (2 * dil)
        o, lse = _window_attn(_to_residues(q, dil), _to_residues(k, dil), _to_residues(v, dil), radius)
        outs.append(_from_residues(o, B, dil))
        lses.append(_from_residues(lse, B, dil))
    w = jax.nn.softmax(jnp.stack(lses, axis=-1), axis=-1)
    out = sum(w[..., i, None].astype(q.dtype) * outs[i] for i in range(len(A_PATTERNS)))
    return out.reshape(B, S, H * dh)


def _gqa_axial(q, k, v, q_norm, k_norm):
    B, S, Hq, dh = q.shape
    q = _rmsnorm(q, q_norm)
    k = _rmsnorm(k, k_norm)
    t = jnp.arange(S)
    f = _freqs(dh // 2, AXIAL_THETA)
    ang_row = (t // GRID_W).astype(jnp.float32)[:, None] * f
    ang_col = (t % GRID_W).astype(jnp.float32)[:, None] * f
    half = dh // 2
    q = jnp.concatenate([_rotate(q[..., :half], ang_row), _rotate(q[..., half:], ang_col)], axis=-1)
    k = jnp.concatenate([_rotate(k[..., :half], ang_row), _rotate(k[..., half:], ang_col)], axis=-1)
    rep = Hq // B_KV_HEADS
    nqb = S // B_QBLOCK
    qb = q.reshape(B, nqb, B_QBLOCK, B_KV_HEADS, rep, dh).transpose(1, 0, 2, 3, 4, 5)
    scale = dh ** -0.5

    def block(qi):
        s = jnp.einsum('bqgrd,bkgd->bgrqk', qi, k).astype(jnp.float32) * scale
        p = jax.nn.softmax(s, axis=-1).astype(v.dtype)
        return jnp.einsum('bgrqk,bkgd->bqgrd', p, v)

    o = lax.map(block, qb)
    return o.transpose(1, 0, 2, 3, 4, 5).reshape(B, S, Hq * dh)


def _neighbourhood_attn(q, k, v, rel_bias):
    B, S, H, dh = q.shape
    R = S // GRID_W
    kh = min(NA_ROWS, R)
    rows = jnp.arange(R)
    rs = jnp.clip(rows - NA_ROWS // 2, 0, R - kh)
    krow = rs[:, None] + jnp.arange(kh)[None, :]
    cols = jnp.arange(GRID_W)
    cs = jnp.clip(cols - NA_COLS // 2, 0, GRID_W - NA_COLS)
    col_valid = (cols[None, :] >= cs[:, None]) & (cols[None, :] < cs[:, None] + NA_COLS)
    qg = q.reshape(B, R, GRID_W, H, dh)
    kg = k.reshape(B, R, GRID_W, H, dh)[:, krow]
    vg = v.reshape(B, R, GRID_W, H, dh)[:, krow]
    s = jnp.einsum('brqhd,brikhd->brhqik', qg, kg).astype(jnp.float32) * (dh ** -0.5)
    ro = krow - rows[:, None] + (NA_ROWS - 1)
    co = jnp.clip(cols[None, :] - cols[:, None] + (NA_COLS - 1), 0, 2 * NA_COLS - 2)
    bias = rel_bias[:, ro[:, None, :, None], co[None, :, None, :]]
    s = s + bias.transpose(1, 0, 2, 3, 4)[None].astype(jnp.float32)
    s = jnp.where(col_valid[:, None, :], s, -jnp.inf)
    p = jax.nn.softmax(s.reshape(s.shape[:4] + (kh * GRID_W,)), axis=-1)
    p = p.reshape(s.shape).astype(v.dtype)
    o = jnp.einsum('brhqik,brikhd->brqhd', p, vg)
    return o.reshape(B, S, H * dh)


def _layer(x, norm_pre, w_in, q_norm, k_norm, rel_bias, branch_gain, w_out, norm_post):
    B, S, _ = x.shape
    h = _rmsnorm(x, norm_pre)
    proj = h @ w_in
    widths = [A_W] * 4 + [B_W, B_KV_W, B_KV_W, B_W] + [C_W] * 4
    cuts = [int(c) for c in np.cumsum(widths)[:-1]]
    aq, ak, av, az, bq, bk, bv, bz, cq, ck, cv, cz = jnp.split(proj, cuts, axis=-1)
    heads = lambda t: t.reshape(B, S, t.shape[-1] // HEAD_DIM, HEAD_DIM)

    ang = jnp.arange(S, dtype=jnp.float32)[:, None] * _freqs(ROPE_DIMS, ROPE_THETA)
    aq, ak = heads(aq), heads(ak)
    aq = jnp.concatenate([_rotate(aq[..., :ROPE_DIMS], ang), aq[..., ROPE_DIMS:]], axis=-1)
    ak = jnp.concatenate([_rotate(ak[..., :ROPE_DIMS], ang), ak[..., ROPE_DIMS:]], axis=-1)
    ya = _dilated_attention(aq, ak, heads(av))
    yb = _gqa_axial(heads(bq), heads(bk), heads(bv), q_norm, k_norm)
    yc = _neighbourhood_attn(heads(cq), heads(ck), heads(cv), rel_bias)

    ga, gb, gc = jnp.split(branch_gain, [A_W, A_W + B_W])
    y = jnp.concatenate([
        _rmsnorm(ya * jax.nn.silu(az), ga),
        _rmsnorm(yb * jax.nn.silu(bz), gb),
        _rmsnorm(yc * jax.nn.silu(cz), gc),
    ], axis=-1)
    return x + _rmsnorm(y @ w_out, norm_post)


def setup_inputs(seed: int = 0) -> dict:
    key = jax.random.key(seed)
    ks = jax.random.split(key, 10)
    f32 = jnp.float32
    nrm = jax.random.normal
    return {
        "x_prompt": nrm(ks[0], (BATCH, SEQ, D_MODEL), f32),
        "x_sample": nrm(ks[1], (DEC_BATCH, DEC_SEQ, D_MODEL), f32),
        "norm_pre": 1.0 + 0.05 * nrm(ks[2], (DEPTH, D_MODEL), f32),
        "w_in": nrm(ks[3], (DEPTH, D_MODEL, IN_W), f32) * (D_MODEL ** -0.5),
        "q_norm": 1.0 + 0.05 * nrm(ks[4], (DEPTH, HEAD_DIM), f32),
        "k_norm": 1.0 + 0.05 * nrm(ks[5], (DEPTH, HEAD_DIM), f32),
        "rel_bias": 0.1 * nrm(ks[6], (DEPTH, C_HEADS, 2 * NA_ROWS - 1, 2 * NA_COLS - 1), f32),
        "branch_gain": 1.0 + 0.05 * nrm(ks[7], (DEPTH, MIX_W), f32),
        "w_out": nrm(ks[8], (DEPTH, MIX_W, D_MODEL), f32) * (MIX_W ** -0.5),
        "norm_post": 1.0 + 0.05 * nrm(ks[9], (DEPTH, D_MODEL), f32),
    }


def reference(x_prompt, x_sample, norm_pre, w_in, q_norm, k_norm, rel_bias, branch_gain, w_out, norm_post):
    y_prompt = x_prompt
    y_sample = x_sample
    for l in range(DEPTH):
        params = (norm_pre[l], w_in[l], q_norm[l], k_norm[l], rel_bias[l], branch_gain[l], w_out[l], norm_post[l])
        y_prompt = _layer(y_prompt, *params)
        y_sample = _layer(y_sample, *params)
    return (y_prompt, y_sample)
```

```python
import functools

import jax
import jax.numpy as jnp
from jax import lax
from jax.experimental import pallas as pl
from jax.experimental.pallas import tpu as pltpu

D_MODEL = 1024
HEAD_DIM = 64
GRID_W = 64
EPS = 1e-6
A_HEADS = 6
A_PATTERNS = ((128, 1), (512, 4), (2048, 16))
ROPE_THETA = 500000.0
ROPE_DIMS = HEAD_DIM // 4
B_HEADS = 4
B_KV_HEADS = 2
AXIAL_THETA = 10000.0
C_HEADS = 6
NA_ROWS = 8
NA_COLS = 16

A_W = A_HEADS * HEAD_DIM
B_W = B_HEADS * HEAD_DIM
B_KV_W = B_KV_HEADS * HEAD_DIM
C_W = C_HEADS * HEAD_DIM
MIX_W = A_W + B_W + C_W
QKV_W = 3 * A_W

LANES = 128
Q_SCALE = HEAD_DIM ** -0.5
VMEM_LIMIT = 56 * 1024 * 1024

ROW_TILE = 512
A_Q_BLOCK = 512
A_SUB = 128
B_TQ = 256
B_TK = 512
C_ROWS = 4
C_WIN = 12
C_TQ = C_ROWS * GRID_W
C_NK = C_WIN * GRID_W
NEG_INF = float("-inf")

_F32 = jnp.float32
_BF16 = jnp.bfloat16


def _cparams(n_grid):
    return pltpu.CompilerParams(dimension_semantics=("arbitrary",) * n_grid,
                                vmem_limit_bytes=VMEM_LIMIT)


def _lane_lo(shape):
    return lax.broadcasted_iota(jnp.int32, shape, len(shape) - 1) % LANES < HEAD_DIM


def _stack_heads(q):
    lo = _lane_lo(q.shape)
    zero = jnp.zeros_like(q)
    return jnp.concatenate([jnp.where(lo, q, zero), jnp.where(lo, zero, q)], axis=0)


def _unstack_heads(o2):
    n = o2.shape[0] // 2
    return jnp.where(_lane_lo((n, o2.shape[1])), o2[:n], o2[n:])


def _dot_nt(a, b):
    return lax.dot_general(a, b, (((1,), (1,)), ((), ())), preferred_element_type=_F32)


def _rotate_tile(t, tab_ref, shift):
    up = pltpu.roll(t, LANES - shift, axis=1)
    dn = pltpu.roll(t, shift, axis=1)
    return t * tab_ref[0] + up * tab_ref[1] + dn * tab_ref[2]


def _inproj_kernel(x_ref, g_ref, w_ref, ta_ref, tb_ref, gq_ref, gk_ref, bd_ref,
                   a1_ref, a4_ref, a16_ref, c_ref, bq_ref, bkv_ref, z_ref, scr_ref):
    x = x_ref[...]
    r = lax.rsqrt(jnp.mean(x * x, axis=-1, keepdims=True) + EPS)
    h = (x * r * g_ref[...]).astype(_BF16)

    pa = jnp.dot(h, w_ref[:, 0:QKV_W], preferred_element_type=_F32)
    for j in range(QKV_W // LANES):
        t = pa[:, j * LANES:(j + 1) * LANES]
        if j < 2 * A_W // LANES:
            t = _rotate_tile(t, ta_ref, ROPE_DIMS // 2)
        if j < A_W // LANES:
            t = t * Q_SCALE
        a1_ref[:, j * LANES:(j + 1) * LANES] = t.astype(_BF16)
        scr_ref[j] = t
    tm = x.shape[0]
    for dil, ref in ((4, a4_ref), (16, a16_ref)):
        for res in range(dil):
            for j in range(QKV_W // LANES):
                rows = scr_ref[j, pl.ds(res, tm // dil, stride=dil), :]
                c0 = res * QKV_W + j * LANES
                ref[:, c0:c0 + LANES] = rows.astype(_BF16)

    pc = jnp.dot(h, w_ref[:, QKV_W:2 * QKV_W], preferred_element_type=_F32)
    c_ref[:, 0:C_W] = (pc[:, 0:C_W] * Q_SCALE).astype(_BF16)
    c_ref[:, C_W:QKV_W] = pc[:, C_W:QKV_W].astype(_BF16)

    o = 2 * QKV_W
    pb = jnp.dot(h, w_ref[:, o:o + 3 * B_W], preferred_element_type=_F32)
    bd = bd_ref[...]

    def headnorm(t, gain):
        sq = t * t
        hi = sq.astype(_BF16)
        lo = (sq - hi.astype(_F32)).astype(_BF16)
        ms = (jnp.dot(hi, bd, preferred_element_type=_F32)
              + jnp.dot(lo, bd, preferred_element_type=_F32))
        return t * lax.rsqrt(ms + EPS) * gain

    for j in range(B_W // LANES):
        t = headnorm(pb[:, j * LANES:(j + 1) * LANES], gq_ref[...])
        t = _rotate_tile(t, tb_ref, HEAD_DIM // 4) * Q_SCALE
        bq_ref[:, j * LANES:(j + 1) * LANES] = t.astype(_BF16)
    for j in range(B_W // LANES):
        t = headnorm(pb[:, B_W + j * LANES:B_W + (j + 1) * LANES], gk_ref[...])
        t = _rotate_tile(t, tb_ref, HEAD_DIM // 4)
        bkv_ref[:, j * LANES:(j + 1) * LANES] = t.astype(_BF16)
    bkv_ref[:, B_W:2 * B_W] = pb[:, 2 * B_W:3 * B_W].astype(_BF16)

    o = 2 * QKV_W + 3 * B_W
    z_ref[...] = jnp.dot(h, w_ref[:, o:o + MIX_W], preferred_element_type=_F32)


def _inproj(x2d, seq, norm_pre, w, tab_a, tab_b, gq, gk, bd):
    T = x2d.shape[0]
    tm = min(ROW_TILE, seq)
    n_seq = seq // tm
    nw = w.shape[1]
    const = lambda i: (0, 0)
    row = lambda i: (i, 0)
    tab = lambda i: (0, i % n_seq, 0)
    out_shape = (
        jax.ShapeDtypeStruct((T, QKV_W), _BF16),
        jax.ShapeDtypeStruct((T // 4, 4 * QKV_W), _BF16),
        jax.ShapeDtypeStruct((T // 16, 16 * QKV_W), _BF16),
        jax.ShapeDtypeStruct((T, QKV_W), _BF16),
        jax.ShapeDtypeStruct((T, B_W), _BF16),
        jax.ShapeDtypeStruct((T, 2 * B_W), _BF16),
        jax.ShapeDtypeStruct((T, MIX_W), _F32),
    )
    out_specs = (
        pl.BlockSpec((tm, QKV_W), row),
        pl.BlockSpec((tm // 4, 4 * QKV_W), row),
        pl.BlockSpec((tm // 16, 16 * QKV_W), row),
        pl.BlockSpec((tm, QKV_W), row),
        pl.BlockSpec((tm, B_W), row),
        pl.BlockSpec((tm, 2 * B_W), row),
        pl.BlockSpec((tm, MIX_W), row),
    )
    in_specs = [
        pl.BlockSpec((tm, D_MODEL), row),
        pl.BlockSpec((1, D_MODEL), const),
        pl.BlockSpec((D_MODEL, nw), const),
        pl.BlockSpec((3, tm, LANES), tab),
        pl.BlockSpec((3, tm, LANES), tab),
        pl.BlockSpec((1, LANES), const),
        pl.BlockSpec((1, LANES), const),
        pl.BlockSpec((LANES, LANES), const),
    ]
    return pl.pallas_call(
        _inproj_kernel,
        grid=(T // tm,),
        in_specs=in_specs,
        out_specs=out_specs,
        out_shape=out_shape,
        scratch_shapes=[pltpu.VMEM((QKV_W // LANES, tm, LANES), _F32)],
        compiler_params=_cparams(1),
        name="inproj",
    )(x2d, norm_pre, w, tab_a, tab_b, gq, gk, bd)


def _attn_a_kernel(q_ref, k_ref, v_ref, o_ref, lse_ref, *, q_block, seq_len, radius, sub):
    nk = min(sub + 2 * radius, seq_len)
    base = pl.program_id(2) * q_block
    row = lax.broadcasted_iota(jnp.int32, (2 * sub, nk), 0) % sub
    col = lax.broadcasted_iota(jnp.int32, (2 * sub, nk), 1)
    delta = col - row

    def body(si, carry):
        l0 = base + si * sub
        kstart = pl.multiple_of(jnp.clip(l0 - radius, 0, seq_len - nk), HEAD_DIM)
        rel = delta + (kstart - l0)
        valid = (rel >= -radius) & (rel <= radius)
        qs = pl.multiple_of(si * sub, sub)
        for p in range(A_W // LANES):
            cols = slice(p * LANES, (p + 1) * LANES)
            q2 = _stack_heads(q_ref[pl.ds(qs, sub), cols])
            kp = k_ref[pl.ds(kstart, nk), cols]
            vp = v_ref[pl.ds(kstart, nk), cols]
            s = jnp.where(valid, _dot_nt(q2, kp), NEG_INF)
            m = jnp.max(s, axis=1, keepdims=True)
            e = jnp.exp(s - m)
            l = jnp.sum(e, axis=1, keepdims=True)
            o2 = jnp.dot(e.astype(_BF16), vp, preferred_element_type=_F32) / l
            lse = jnp.broadcast_to(m + jnp.log(l), o2.shape)
            o_ref[pl.ds(qs, sub), cols] = _unstack_heads(o2)
            lse_ref[pl.ds(qs, sub), cols] = _unstack_heads(lse)
        return carry

    lax.fori_loop(0, q_block // sub, body, 0)


def _attn_a(a_view, batch, seq, dil, radius):
    L = seq // dil
    tl = min(A_Q_BLOCK, L)
    sub = min(A_SUB, tl)
    n_lt = L // tl
    rows = a_view.shape[0]
    kern = functools.partial(_attn_a_kernel, q_block=tl, seq_len=L, radius=radius, sub=sub)
    n_grp = QKV_W // A_W
    q_spec = pl.BlockSpec((tl, A_W), lambda b, r, t: (b * n_lt + t, r * n_grp))
    k_spec = pl.BlockSpec((L, A_W), lambda b, r, t: (b, r * n_grp + 1))
    v_spec = pl.BlockSpec((L, A_W), lambda b, r, t: (b, r * n_grp + 2))
    o_spec = pl.BlockSpec((tl, A_W), lambda b, r, t: (b * n_lt + t, r))
    out_shape = (jax.ShapeDtypeStruct((rows, dil * A_W), _F32),) * 2
    return pl.pallas_call(
        kern,
        grid=(batch, dil, n_lt),
        in_specs=[q_spec, k_spec, v_spec],
        out_specs=(o_spec, o_spec),
        out_shape=out_shape,
        compiler_params=_cparams(3),
        name=f"attn_a_d{dil}",
    )(a_view, a_view, a_view)


def _attn_b_kernel(q_ref, k_ref, v_ref, o_ref, *, seq, tk):
    q2 = _stack_heads(q_ref[...])
    n2 = q2.shape[0]

    def body(kt, carry):
        m, l, acc = carry
        ks = pl.multiple_of(kt * tk, tk)
        s = _dot_nt(q2, k_ref[pl.ds(ks, tk), :])
        m_new = jnp.maximum(m, jnp.max(s, axis=1, keepdims=True))
        alpha = jnp.exp(m - m_new)
        p = jnp.exp(s - m_new)
        l = alpha * l + jnp.sum(p, axis=1, keepdims=True)
        acc = alpha * acc + jnp.dot(p.astype(_BF16), v_ref[pl.ds(ks, tk), :],
                                    preferred_element_type=_F32)
        return m_new, l, acc

    init = (jnp.full((n2, 1), NEG_INF, _F32), jnp.zeros((n2, 1), _F32),
            jnp.zeros((n2, LANES), _F32))
    _, l, acc = lax.fori_loop(0, seq // tk, body, init)
    o_ref[...] = _unstack_heads(acc / l)


def _attn_b(bq, bkv, batch, seq):
    T = bq.shape[0]
    tq = min(B_TQ, seq)
    tk = min(B_TK, seq)
    nq = seq // tq
    n_grp = B_W // LANES
    kern = functools.partial(_attn_b_kernel, seq=seq, tk=tk)
    return pl.pallas_call(
        kern,
        grid=(batch, n_grp, nq),
        in_specs=[
            pl.BlockSpec((tq, LANES), lambda b, g, t: (b * nq + t, g)),
            pl.BlockSpec((seq, LANES), lambda b, g, t: (b, g)),
            pl.BlockSpec((seq, LANES), lambda b, g, t: (b, n_grp + g)),
        ],
        out_specs=pl.BlockSpec((tq, LANES), lambda b, g, t: (b * nq + t, g)),
        out_shape=jax.ShapeDtypeStruct((T, B_W), _F32),
        compiler_params=_cparams(3),
        name="attn_b",
    )(bq, bkv, bkv)


def _c_tile_types():
    half = NA_ROWS // 2
    first = dict(ro_base=NA_ROWS - 1, start=[0] * C_ROWS)
    inner = dict(ro_base=NA_ROWS - 1 - half, start=list(range(C_ROWS)))
    last = dict(ro_base=NA_ROWS - 1 - (C_WIN - C_ROWS), start=[C_WIN - NA_ROWS] * C_ROWS)
    return (first, inner, last)


def _bias_kernel(rb_ref, out_ref):
    h = pl.program_id(0)
    n_ro = 2 * NA_ROWS - 1
    n_co = 2 * NA_COLS - 1
    qi = lax.broadcasted_iota(jnp.int32, (GRID_W, LANES), 0)
    lane = lax.broadcasted_iota(jnp.int32, (GRID_W, LANES), 1)
    kc = lane % GRID_W
    d = kc - qi
    cs = jnp.clip(qi - NA_COLS // 2, 0, GRID_W - NA_COLS)
    col_valid = (kc >= cs) & (kc < cs + NA_COLS)
    hits = [d == co - (NA_COLS - 1) for co in range(n_co)]
    neg = jnp.full((GRID_W, LANES), NEG_INF, _F32)
    rows = []
    for ro in range(n_ro):
        e = neg
        for co in range(n_co):
            e = jnp.where(hits[co], rb_ref[(h * n_ro + ro) * n_co + co], e)
        rows.append(jnp.where(col_valid, e, neg))
    lo = lane < GRID_W
    for t, spec in enumerate(_c_tile_types()):
        for j in range(C_ROWS):
            def blk(kk):
                ok = spec["start"][j] <= kk < spec["start"][j] + NA_ROWS
                return rows[kk - j + spec["ro_base"]] if ok else neg
            tiles = [jnp.where(lo, blk(2 * i), blk(2 * i + 1)) for i in range(C_WIN // 2)]
            out_ref[t, j * GRID_W:(j + 1) * GRID_W, :] = jnp.concatenate(tiles, axis=1)


def _bias_table(rel_bias):
    flat = rel_bias.reshape(-1)
    return pl.pallas_call(
        _bias_kernel,
        grid=(C_HEADS,),
        in_specs=[pl.BlockSpec(memory_space=pltpu.SMEM)],
        out_specs=pl.BlockSpec((3, None, C_TQ, C_NK), lambda h: (0, h, 0, 0)),
        out_shape=jax.ShapeDtypeStruct((3, C_HEADS, C_TQ, C_NK), _F32),
        compiler_params=_cparams(1),
        name="c_bias",
    )(flat)


def _attn_c_kernel(q_ref, k_ref, v_ref, b_ref, o_ref, *, grid_rows):
    r0 = pl.program_id(2) * C_ROWS
    ws = jnp.clip(r0 - NA_ROWS // 2, 0, grid_rows - C_WIN)
    ks = pl.multiple_of(ws * GRID_W, GRID_W)
    q2 = _stack_heads(q_ref[...])
    s = _dot_nt(q2, k_ref[pl.ds(ks, C_NK), :]) + b_ref[...]
    m = jnp.max(s, axis=1, keepdims=True)
    e = jnp.exp(s - m)
    l = jnp.sum(e, axis=1, keepdims=True)
    o2 = jnp.dot(e.astype(_BF16), v_ref[pl.ds(ks, C_NK), :], preferred_element_type=_F32) / l
    o_ref[...] = _unstack_heads(o2)


def _attn_c(cqkv, bias, batch, seq):
    T = cqkv.shape[0]
    grid_rows = seq // GRID_W
    assert grid_rows >= C_WIN and grid_rows % C_ROWS == 0
    nq = seq // C_TQ
    n_pair = C_W // LANES
    bias = bias.reshape(3, n_pair, 2 * C_TQ, C_NK)

    def tile_type(t):
        return jnp.where(t == 0, 0, jnp.where(t == nq - 1, 2, 1))

    kern = functools.partial(_attn_c_kernel, grid_rows=grid_rows)
    return pl.pallas_call(
        kern,
        grid=(batch, n_pair, nq),
        in_specs=[
            pl.BlockSpec((C_TQ, LANES), lambda b, p, t: (b * nq + t, p)),
            pl.BlockSpec((seq, LANES), lambda b, p, t: (b, n_pair + p)),
            pl.BlockSpec((seq, LANES), lambda b, p, t: (b, 2 * n_pair + p)),
            pl.BlockSpec((None, None, 2 * C_TQ, C_NK), lambda b, p, t: (tile_type(t), p, 0, 0)),
        ],
        out_specs=pl.BlockSpec((C_TQ, LANES), lambda b, p, t: (b * nq + t, p)),
        out_shape=jax.ShapeDtypeStruct((T, C_W), _F32),
        compiler_params=_cparams(3),
        name="attn_c",
    )(cqkv, cqkv, cqkv, bias)


def _gated_norm(y, z, gain):
    u = y * (z * jax.nn.sigmoid(z))
    r = lax.rsqrt(jnp.mean(u * u, axis=-1, keepdims=True) + EPS)
    return u * r * gain


def _outproj_kernel(o1_ref, o2_ref, o3_ref, l1_ref, l2_ref, l3_ref, yb_ref, yc_ref, z_ref,
                    gain_ref, w_ref, npost_ref, x_ref, out_ref):
    l1, l2, l3 = l1_ref[...], l2_ref[...], l3_ref[...]
    m = jnp.maximum(jnp.maximum(l1, l2), l3)
    e1, e2, e3 = jnp.exp(l1 - m), jnp.exp(l2 - m), jnp.exp(l3 - m)
    den = e1 + e2 + e3
    ya = (e1 / den) * o1_ref[...] + (e2 / den) * o2_ref[...] + (e3 / den) * o3_ref[...]
    z = z_ref[...]
    gain = gain_ref[...]
    y = jnp.concatenate([
        _gated_norm(ya, z[:, 0:A_W], gain[:, 0:A_W]),
        _gated_norm(yb_ref[...], z[:, A_W:A_W + B_W], gain[:, A_W:A_W + B_W]),
        _gated_norm(yc_ref[...], z[:, A_W + B_W:MIX_W], gain[:, A_W + B_W:MIX_W]),
    ], axis=1).astype(_BF16)
    t = jnp.dot(y, w_ref[...], preferred_element_type=_F32)
    r = lax.rsqrt(jnp.mean(t * t, axis=-1, keepdims=True) + EPS)
    out_ref[...] = x_ref[...] + t * r * npost_ref[...]


def _outproj(os_, lses, yb, yc, z, gain, w, npost, x2d):
    T = x2d.shape[0]
    tm = min(ROW_TILE, T)
    row = lambda i: (i, 0)
    const = lambda i: (0, 0)
    in_specs = ([pl.BlockSpec((tm, A_W), row)] * 6
                + [pl.BlockSpec((tm, B_W), row), pl.BlockSpec((tm, C_W), row),
                   pl.BlockSpec((tm, MIX_W), row), pl.BlockSpec((1, MIX_W), const),
                   pl.BlockSpec((MIX_W, D_MODEL), const), pl.BlockSpec((1, D_MODEL), const),
                   pl.BlockSpec((tm, D_MODEL), row)])
    return pl.pallas_call(
        _outproj_kernel,
        grid=(T // tm,),
        in_specs=in_specs,
        out_specs=pl.BlockSpec((tm, D_MODEL), row),
        out_shape=jax.ShapeDtypeStruct((T, D_MODEL), _F32),
        compiler_params=_cparams(1),
        name="outproj",
    )(*os_, *lses, yb, yc, z, gain, w, npost, x2d)


def _pair_table(per_head):
    return jnp.concatenate([per_head, per_head], axis=1)


def _rope_tables(seq):
    n = ROPE_DIMS // 2
    freqs = ROPE_THETA ** (-jnp.arange(0, ROPE_DIMS, 2, dtype=_F32) / ROPE_DIMS)
    ang = jnp.arange(seq, dtype=_F32)[:, None] * freqs
    cos, sin = jnp.cos(ang), jnp.sin(ang)
    ones = jnp.ones((seq, HEAD_DIM - ROPE_DIMS), _F32)
    zeros = jnp.zeros((seq, HEAD_DIM - ROPE_DIMS), _F32)
    zn = jnp.zeros((seq, n), _F32)
    c = jnp.concatenate([cos, cos, ones], axis=1)
    s_up = jnp.concatenate([-sin, zn, zeros], axis=1)
    s_dn = jnp.concatenate([zn, sin, zeros], axis=1)
    return jnp.stack([_pair_table(c), _pair_table(s_up), _pair_table(s_dn)])


def _axial_tables(seq):
    half = HEAD_DIM // 2
    freqs = AXIAL_THETA ** (-jnp.arange(0, half, 2, dtype=_F32) / half)
    t = jnp.arange(seq)
    ang_row = (t // GRID_W).astype(_F32)[:, None] * freqs
    ang_col = (t % GRID_W).astype(_F32)[:, None] * freqs
    zn = jnp.zeros((seq, half // 2), _F32)
    parts_c, parts_up, parts_dn = [], [], []
    for ang in (ang_row, ang_col):
        cos, sin = jnp.cos(ang), jnp.sin(ang)
        parts_c += [cos, cos]
        parts_up += [-sin, zn]
        parts_dn += [zn, sin]
    cat = lambda parts: _pair_table(jnp.concatenate(parts, axis=1))
    return jnp.stack([cat(parts_c), cat(parts_up), cat(parts_dn)])


def _pack_w_in(w_in):
    cuts = {}
    off = 0
    for name, width in (("aq", A_W), ("ak", A_W), ("av", A_W), ("az", A_W), ("bq", B_W), ("bk", B_KV_W),
                        ("bv", B_KV_W), ("bz", B_W), ("cq", C_W), ("ck", C_W), ("cv", C_W), ("cz", C_W)):
        cuts[name] = w_in[..., off:off + width]
        off += width

    def dup(w):
        lead = w.shape[:-1]
        w = w.reshape(lead + (B_KV_HEADS, 1, HEAD_DIM))
        return jnp.broadcast_to(w, lead + (B_KV_HEADS, 2, HEAD_DIM)).reshape(lead + (2 * B_KV_W,))

    order = [cuts["aq"], cuts["ak"], cuts["av"], cuts["cq"], cuts["ck"], cuts["cv"], cuts["bq"],
             dup(cuts["bk"]), dup(cuts["bv"]), cuts["az"], cuts["bz"], cuts["cz"]]
    return jnp.concatenate(order, axis=-1).astype(_BF16)


def _layer(x2d, batch, seq, tabs, norm_pre, w_in, gq, gk, bd, bias, gain, w_out, npost):
    tab_a, tab_b = tabs
    a1, a4, a16, cqkv, bq, bkv, z = _inproj(x2d, seq, norm_pre, w_in, tab_a, tab_b, gq, gk, bd)
    T = x2d.shape[0]
    os_, lses = [], []
    for (window, dil), view in zip(A_PATTERNS, (a1, a4, a16)):
        o, lse = _attn_a(view, batch, seq, dil, window // (2 * dil))
        os_.append(o.reshape(T, A_W))
        lses.append(lse.reshape(T, A_W))
    yb = _attn_b(bq, bkv, batch, seq)
    yc = _attn_c(cqkv, bias, batch, seq)
    return _outproj(os_, lses, yb, yc, z, gain, w_out, npost, x2d)


def kernel(x_prompt, x_sample, norm_pre, w_in, q_norm, k_norm, rel_bias, branch_gain, w_out, norm_post):
    depth = w_in.shape[0]
    w_in_p = _pack_w_in(w_in)
    w_out_p = w_out.astype(_BF16)
    bd = jnp.kron(jnp.eye(LANES // HEAD_DIM, dtype=_F32),
                  jnp.full((HEAD_DIM, HEAD_DIM), 1.0 / HEAD_DIM, _F32)).astype(_BF16)
    groups = []
    for x in (x_prompt, x_sample):
        batch, seq, _ = x.shape
        groups.append(dict(y=x.reshape(batch * seq, D_MODEL), batch=batch, seq=seq,
                           tabs=(_rope_tables(seq), _axial_tables(seq))))
    for l in range(depth):
        bias = _bias_table(rel_bias[l])
        gq = _pair_table(q_norm[l][None, :])
        gk = _pair_table(k_norm[l][None, :])
        for g in groups:
            g["y"] = _layer(g["y"], g["batch"], g["seq"], g["tabs"], norm_pre[l][None, :], w_in_p[l],
                            gq, gk, bd, bias, branch_gain[l][None, :], w_out_p[l], norm_post[l][None, :])
    return tuple(g["y"].reshape(g["batch"], g["seq"], D_MODEL) for g in groups)
```

```python
import functools
import math

import jax
import jax.numpy as jnp
from jax import lax
from jax.experimental import pallas as pl
from jax.experimental.pallas import tpu as pltpu

D_MODEL = 1024
HEAD_DIM = 64
GRID_W = 64
EPS = 1e-6
A_HEADS = 6
A_PATTERNS = ((128, 1), (512, 4), (2048, 16))
ROPE_THETA = 500000.0
ROPE_DIMS = HEAD_DIM // 4
B_HEADS = 4
B_KV_HEADS = 2
AXIAL_THETA = 10000.0
C_HEADS = 6
NA_ROWS = 8
NA_COLS = 16

A_W = A_HEADS * HEAD_DIM
B_W = B_HEADS * HEAD_DIM
B_KV_W = B_KV_HEADS * HEAD_DIM
C_W = C_HEADS * HEAD_DIM
MIX_W = A_W + B_W + C_W
QKV_W = 3 * A_W

LANES = 128
A_TILES = A_W // LANES
Q_SCALE = HEAD_DIM ** -0.5
LOG2E = math.log2(math.e)
VMEM_LIMIT = 56 * 1024 * 1024

ROW_TILE = 512
A_SUB = 128
A_STEP_TOKENS = 2048
A1_Q_BLOCK = 512
B_TQ = 512
B_TK = 2048
C_ROWS = 4
C_WIN = 12
C_TQ = C_ROWS * GRID_W
C_STEP_TILES = 2
C_NK = C_WIN * GRID_W
NEG_INF = float("-inf")

_F32 = jnp.float32
_BF16 = jnp.bfloat16


def _cparams(n_grid):
    return pltpu.CompilerParams(dimension_semantics=("arbitrary",) * n_grid,
                                vmem_limit_bytes=VMEM_LIMIT)


def _lane_lo(shape):
    return lax.broadcasted_iota(jnp.int32, shape, len(shape) - 1) % LANES < HEAD_DIM


def _stack_heads(q):
    lo = _lane_lo(q.shape)
    zero = jnp.zeros_like(q)
    return jnp.concatenate([jnp.where(lo, q, zero), jnp.where(lo, zero, q)], axis=0)


def _unstack_heads(o2):
    n = o2.shape[0] // 2
    return jnp.where(_lane_lo((n, o2.shape[1])), o2[:n], o2[n:])


def _dot_nt(a, b):
    return lax.dot_general(a, b, (((1,), (1,)), ((), ())), preferred_element_type=_F32)


def _gated_norm(y, z, gain):
    z = z.astype(_F32)
    u = y * (z * jax.nn.sigmoid(z))
    r = lax.rsqrt(jnp.mean(u * u, axis=-1, keepdims=True) + EPS)
    return u * r * gain


def _rotate_tile(t, tab_ref, shift):
    up = pltpu.roll(t, LANES - shift, axis=1)
    dn = pltpu.roll(t, shift, axis=1)
    return t * tab_ref[0] + up * tab_ref[1] + dn * tab_ref[2]


def _inproj_kernel(x_ref, g_ref, w_ref, ta_ref, tb_ref, gq_ref, gk_ref, bd_ref,
                   a1_ref, a4_ref, a16_ref, c_ref, bq_ref, bkv_ref, az_ref, bz_ref, cz_ref, scr_ref):
    x = x_ref[...]
    r = lax.rsqrt(jnp.mean(x * x, axis=-1, keepdims=True) + EPS)
    h = (x * r * g_ref[...]).astype(_BF16)

    pa = jnp.dot(h, w_ref[:, 0:QKV_W], preferred_element_type=_F32)
    for j in range(QKV_W // LANES):
        t = pa[:, j * LANES:(j + 1) * LANES]
        if j < 2 * A_TILES:
            t = _rotate_tile(t, ta_ref, ROPE_DIMS // 2)
        if j < A_TILES:
            t = t * Q_SCALE
        a1_ref[:, j * LANES:(j + 1) * LANES] = t.astype(_BF16)
        scr_ref[j] = t
    tm = x.shape[0]
    for dil, ref in ((4, a4_ref), (16, a16_ref)):
        for res in range(dil):
            for j in range(QKV_W // LANES):
                rows = scr_ref[j, pl.ds(res, tm // dil, stride=dil), :]
                ref[res, :, j * LANES:(j + 1) * LANES] = rows.astype(_BF16)

    pc = jnp.dot(h, w_ref[:, QKV_W:2 * QKV_W], preferred_element_type=_F32)
    c_ref[:, 0:C_W] = (pc[:, 0:C_W] * Q_SCALE).astype(_BF16)
    c_ref[:, C_W:QKV_W] = pc[:, C_W:QKV_W].astype(_BF16)

    o = 2 * QKV_W
    pb = jnp.dot(h, w_ref[:, o:o + 3 * B_W], preferred_element_type=_F32)
    bd = bd_ref[...]

    def headnorm(t, gain):
        sq = t * t
        hi = sq.astype(_BF16)
        lo = (sq - hi.astype(_F32)).astype(_BF16)
        ms = (jnp.dot(hi, bd, preferred_element_type=_F32)
              + jnp.dot(lo, bd, preferred_element_type=_F32))
        return t * lax.rsqrt(ms + EPS) * gain

    for j in range(B_W // LANES):
        t = headnorm(pb[:, j * LANES:(j + 1) * LANES], gq_ref[...])
        t = _rotate_tile(t, tb_ref, HEAD_DIM // 4) * (Q_SCALE * LOG2E)
        bq_ref[:, j * LANES:(j + 1) * LANES] = t.astype(_BF16)
    for j in range(B_W // LANES):
        t = headnorm(pb[:, B_W + j * LANES:B_W + (j + 1) * LANES], gk_ref[...])
        t = _rotate_tile(t, tb_ref, HEAD_DIM // 4)
        bkv_ref[:, j * LANES:(j + 1) * LANES] = t.astype(_BF16)
    lo_half = _lane_lo((tm, LANES))
    for j in range(B_W // LANES):
        t = pb[:, 2 * B_W + j * LANES:2 * B_W + (j + 1) * LANES]
        bkv_ref[:, B_W + j * LANES:B_W + (j + 1) * LANES] = jnp.where(lo_half, t, 1.0).astype(_BF16)

    o = 2 * QKV_W + 3 * B_W
    pz = jnp.dot(h, w_ref[:, o:o + MIX_W], preferred_element_type=_F32)
    az_ref[...] = pz[:, 0:A_W].astype(_BF16)
    bz_ref[...] = pz[:, A_W:A_W + B_W].astype(_BF16)
    cz_ref[...] = pz[:, A_W + B_W:MIX_W].astype(_BF16)


def _inproj(x2d, seq, norm_pre, w, tab_a, tab_b, gq, gk, bd):
    T = x2d.shape[0]
    tm = min(ROW_TILE, seq)
    n_seq = seq // tm
    nw = w.shape[1]
    const = lambda i: (0, 0)
    row = lambda i: (i, 0)
    row3 = lambda i: (0, i, 0)
    tab = lambda i: (0, i % n_seq, 0)
    out_shape = (
        jax.ShapeDtypeStruct((T, QKV_W), _BF16),
        jax.ShapeDtypeStruct((4, T // 4, QKV_W), _BF16),
        jax.ShapeDtypeStruct((16, T // 16, QKV_W), _BF16),
        jax.ShapeDtypeStruct((T, QKV_W), _BF16),
        jax.ShapeDtypeStruct((T, B_W), _BF16),
        jax.ShapeDtypeStruct((T, 2 * B_W), _BF16),
        jax.ShapeDtypeStruct((T, A_W), _BF16),
        jax.ShapeDtypeStruct((T, B_W), _BF16),
        jax.ShapeDtypeStruct((T, C_W), _BF16),
    )
    out_specs = (
        pl.BlockSpec((tm, QKV_W), row),
        pl.BlockSpec((4, tm // 4, QKV_W), row3),
        pl.BlockSpec((16, tm // 16, QKV_W), row3),
        pl.BlockSpec((tm, QKV_W), row),
        pl.BlockSpec((tm, B_W), row),
        pl.BlockSpec((tm, 2 * B_W), row),
        pl.BlockSpec((tm, A_W), row),
        pl.BlockSpec((tm, B_W), row),
        pl.BlockSpec((tm, C_W), row),
    )
    in_specs = [
        pl.BlockSpec((tm, D_MODEL), row),
        pl.BlockSpec((1, D_MODEL), const),
        pl.BlockSpec((D_MODEL, nw), const, pipeline_mode=pl.Buffered(1)),
        pl.BlockSpec((3, tm, LANES), tab),
        pl.BlockSpec((3, tm, LANES), tab),
        pl.BlockSpec((1, LANES), const),
        pl.BlockSpec((1, LANES), const),
        pl.BlockSpec((LANES, LANES), const),
    ]
    return pl.pallas_call(
        _inproj_kernel,
        grid=(T // tm,),
        in_specs=in_specs,
        out_specs=out_specs,
        out_shape=out_shape,
        scratch_shapes=[pltpu.VMEM((QKV_W // LANES, tm, LANES), _F32)],
        compiler_params=_cparams(1),
        name="inproj",
    )(x2d, norm_pre, w, tab_a, tab_b, gq, gk, bd)


def _band_tile(q, k_ref, v_ref, kidx, kstart, nk, cols, valid):
    q2 = _stack_heads(q)
    kp = k_ref[kidx + (pl.ds(kstart, nk), cols)]
    vp = v_ref[kidx + (pl.ds(kstart, nk), cols)]
    s = jnp.where(valid, _dot_nt(q2, kp), NEG_INF)
    m = jnp.max(s, axis=1, keepdims=True)
    e = jnp.exp(s - m)
    l = jnp.sum(e, axis=1, keepdims=True)
    o2 = jnp.dot(e.astype(_BF16), vp, preferred_element_type=_F32) / l
    lse = jnp.broadcast_to(m + jnp.log(l), o2.shape)
    return _unstack_heads(o2), _unstack_heads(lse)


def _band_mask(l0, kstart, radius, sub, nk):
    row = lax.broadcasted_iota(jnp.int32, (2 * sub, nk), 0) % sub
    col = lax.broadcasted_iota(jnp.int32, (2 * sub, nk), 1)
    rel = col - row + (kstart - l0)
    return (rel >= -radius) & (rel <= radius)


def _attn_a_dil_kernel(q_ref, k_ref, v_ref, o_ref, lse_ref, *, dil, q_block, seq_len, radius, sub):
    nk = min(sub + 2 * radius, seq_len)
    base = pl.program_id(1) * q_block

    def body(res, carry):
        for si in range(q_block // sub):
            l0 = base + si * sub
            kstart = pl.multiple_of(jnp.clip(l0 - radius, 0, seq_len - nk), HEAD_DIM)
            valid = _band_mask(l0, kstart, radius, sub, nk)
            for p in range(A_TILES):
                cols = slice(p * LANES, (p + 1) * LANES)
                q = q_ref[res, si * sub:(si + 1) * sub, cols]
                o, lse = _band_tile(q, k_ref, v_ref, (res,), kstart, nk, cols, valid)
                rows = pl.ds(si * sub * dil + res, sub, stride=dil)
                o_ref[p, rows, :] = o
                lse_ref[p, rows, :] = lse
        return carry

    lax.fori_loop(0, dil, body, 0, unroll=max(1, min(dil, 4 * sub // q_block)))


def _attn_a_dil(a_res, batch, seq, dil, radius):
    L = seq // dil
    tl = min(max(A_STEP_TOKENS // dil, A_SUB), L)
    sub = min(A_SUB, tl)
    n_lt = L // tl
    T = batch * seq
    kern = functools.partial(_attn_a_dil_kernel, dil=dil, q_block=tl, seq_len=L, radius=radius, sub=sub)
    q_spec = pl.BlockSpec((dil, tl, A_W), lambda b, t: (0, b * n_lt + t, 0))
    k_spec = pl.BlockSpec((dil, L, A_W), lambda b, t: (0, b, 1))
    v_spec = pl.BlockSpec((dil, L, A_W), lambda b, t: (0, b, 2))
    o_spec = pl.BlockSpec((A_TILES, tl * dil, LANES), lambda b, t: (0, b * n_lt + t, 0))
    out_shape = (jax.ShapeDtypeStruct((A_TILES, T, LANES), _F32),) * 2
    return pl.pallas_call(
        kern,
        grid=(batch, n_lt),
        in_specs=[q_spec, k_spec, v_spec],
        out_specs=(o_spec, o_spec),
        out_shape=out_shape,
        compiler_params=_cparams(2),
        name=f"attn_a_d{dil}",
    )(a_res, a_res, a_res)


def _attn_a_final_kernel(q_ref, k_ref, v_ref, o4_ref, l4_ref, o16_ref, l16_ref, z_ref, g_ref, y_ref,
                         *, q_block, seq_len, radius, sub):
    nk = min(sub + 2 * radius, seq_len)
    base = pl.program_id(1) * q_block

    def body(si, carry):
        l0 = base + si * sub
        kstart = pl.multiple_of(jnp.clip(l0 - radius, 0, seq_len - nk), HEAD_DIM)
        valid = _band_mask(l0, kstart, radius, sub, nk)
        qs = pl.multiple_of(si * sub, sub)
        tiles = []
        for p in range(A_TILES):
            cols = slice(p * LANES, (p + 1) * LANES)
            o1, l1 = _band_tile(q_ref[pl.ds(qs, sub), cols], k_ref, v_ref, (), kstart, nk, cols, valid)
            l4, l16 = l4_ref[p, pl.ds(qs, sub), :], l16_ref[p, pl.ds(qs, sub), :]
            m = jnp.maximum(jnp.maximum(l1, l4), l16)
            e1, e4, e16 = jnp.exp(l1 - m), jnp.exp(l4 - m), jnp.exp(l16 - m)
            den = e1 + e4 + e16
            tiles.append((e1 / den) * o1 + (e4 / den) * o4_ref[p, pl.ds(qs, sub), :]
                         + (e16 / den) * o16_ref[p, pl.ds(qs, sub), :])
        ya = jnp.concatenate(tiles, axis=1)
        y_ref[pl.ds(qs, sub), :] = _gated_norm(ya, z_ref[pl.ds(qs, sub), :], g_ref[...]).astype(_BF16)
        return carry

    lax.fori_loop(0, q_block // sub, body, 0, unroll=2)


def _attn_a_final(a1, o4, l4, o16, l16, az, gain, batch, seq, radius):
    T = a1.shape[0]
    tl = min(A1_Q_BLOCK, seq)
    sub = min(A_SUB, tl)
    n_lt = seq // tl
    kern = functools.partial(_attn_a_final_kernel, q_block=tl, seq_len=seq, radius=radius, sub=sub)
    tok = lambda b, t: (b * n_lt + t, 0)
    tok3 = lambda b, t: (0, b * n_lt + t, 0)
    st_spec = pl.BlockSpec((A_TILES, tl, LANES), tok3)
    return pl.pallas_call(
        kern,
        grid=(batch, n_lt),
        in_specs=[
            pl.BlockSpec((tl, A_W), tok),
            pl.BlockSpec((seq, A_W), lambda b, t: (b, 1)),
            pl.BlockSpec((seq, A_W), lambda b, t: (b, 2)),
            st_spec, st_spec, st_spec, st_spec,
            pl.BlockSpec((tl, A_W), tok),
            pl.BlockSpec((1, A_W), lambda b, t: (0, 0)),
        ],
        out_specs=pl.BlockSpec((tl, A_W), tok),
        out_shape=jax.ShapeDtypeStruct((T, A_W), _BF16),
        compiler_params=_cparams(2),
        name="attn_a_d1",
    )(a1, a1, a1, o4, l4, o16, l16, az, gain)


def _attn_b_kernel(q_ref, kv_ref, z_ref, g_ref, y_ref, *, seq, tk):
    n_grp = B_W // LANES
    q2s = [_stack_heads(q_ref[:, g * LANES:(g + 1) * LANES]) for g in range(n_grp)]
    n2 = q2s[0].shape[0]

    def body(kt, carry):
        ks = pl.multiple_of(kt * tk, tk)
        out = []
        for g in range(n_grp):
            m, acc = carry[g]
            k = kv_ref[pl.ds(ks, tk), g * LANES:(g + 1) * LANES]
            v = kv_ref[pl.ds(ks, tk), B_W + g * LANES:B_W + (g + 1) * LANES]
            s = _dot_nt(q2s[g], k)
            m_new = jnp.maximum(m, jnp.max(s, axis=1, keepdims=True))
            alpha = jnp.exp2(m - m_new)
            p = jnp.exp2(s - m_new)
            acc = alpha * acc + jnp.dot(p.astype(_BF16), v, preferred_element_type=_F32)
            out.append((m_new, acc))
        return tuple(out)

    init = tuple((jnp.full((n2, 1), NEG_INF, _F32), jnp.zeros((n2, LANES), _F32)) for _ in range(n_grp))
    final = lax.fori_loop(0, seq // tk, body, init)
    tq = n2 // 2
    tiles = []
    for g in range(n_grp):
        acc = final[g][1]
        on = acc / pltpu.roll(acc, HEAD_DIM, axis=1)
        tiles.append(jnp.where(_lane_lo((tq, LANES)), on[:tq], pltpu.roll(on[tq:], HEAD_DIM, axis=1)))
    yb = jnp.concatenate(tiles, axis=1)
    y_ref[...] = _gated_norm(yb, z_ref[...], g_ref[...]).astype(_BF16)


def _attn_b(bq, bkv, bz, gain, batch, seq):
    T = bq.shape[0]
    tq = min(B_TQ, seq)
    tk = min(B_TK, seq)
    nq = seq // tq
    kern = functools.partial(_attn_b_kernel, seq=seq, tk=tk)
    tok = lambda b, t: (b * nq + t, 0)
    return pl.pallas_call(
        kern,
        grid=(batch, nq),
        in_specs=[
            pl.BlockSpec((tq, B_W), tok),
            pl.BlockSpec((seq, 2 * B_W), lambda b, t: (b, 0)),
            pl.BlockSpec((tq, B_W), tok),
            pl.BlockSpec((1, B_W), lambda b, t: (0, 0)),
        ],
        out_specs=pl.BlockSpec((tq, B_W), tok),
        out_shape=jax.ShapeDtypeStruct((T, B_W), _BF16),
        compiler_params=_cparams(2),
        name="attn_b",
    )(bq, bkv, bz, gain)


def _c_tile_types():
    half = NA_ROWS // 2
    first = dict(ro_base=NA_ROWS - 1, start=[0] * C_ROWS)
    inner = dict(ro_base=NA_ROWS - 1 - half, start=list(range(C_ROWS)))
    last = dict(ro_base=NA_ROWS - 1 - (C_WIN - C_ROWS), start=[C_WIN - NA_ROWS] * C_ROWS)
    return (first, inner, last)


def _bias_kernel(rb_ref, out_ref):
    h = pl.program_id(0)
    n_ro = 2 * NA_ROWS - 1
    n_co = 2 * NA_COLS - 1
    qi = lax.broadcasted_iota(jnp.int32, (GRID_W, LANES), 0)
    lane = lax.broadcasted_iota(jnp.int32, (GRID_W, LANES), 1)
    kc = lane % GRID_W
    d = kc - qi
    cs = jnp.clip(qi - NA_COLS // 2, 0, GRID_W - NA_COLS)
    col_valid = (kc >= cs) & (kc < cs + NA_COLS)
    hits = [d == co - (NA_COLS - 1) for co in range(n_co)]
    neg = jnp.full((GRID_W, LANES), NEG_INF, _F32)
    rows = []
    for ro in range(n_ro):
        e = neg
        for co in range(n_co):
            e = jnp.where(hits[co], rb_ref[(h * n_ro + ro) * n_co + co], e)
        rows.append(jnp.where(col_valid, e, neg))
    lo = lane < GRID_W
    for t, spec in enumerate(_c_tile_types()):
        for j in range(C_ROWS):
            def blk(kk):
                ok = spec["start"][j] <= kk < spec["start"][j] + NA_ROWS
                return rows[kk - j + spec["ro_base"]] if ok else neg
            tiles = [jnp.where(lo, blk(2 * i), blk(2 * i + 1)) for i in range(C_WIN // 2)]
            out_ref[t, j * GRID_W:(j + 1) * GRID_W, :] = jnp.concatenate(tiles, axis=1)


def _bias_table(rel_bias):
    flat = rel_bias.reshape(-1)
    return pl.pallas_call(
        _bias_kernel,
        grid=(C_HEADS,),
        in_specs=[pl.BlockSpec(memory_space=pltpu.SMEM)],
        out_specs=pl.BlockSpec((3, None, C_TQ, C_NK), lambda h: (0, h, 0, 0)),
        out_shape=jax.ShapeDtypeStruct((3, C_HEADS, C_TQ, C_NK), _F32),
        compiler_params=_cparams(1),
        name="c_bias",
    )(flat)


def _attn_c_kernel(q_ref, k_ref, v_ref, *rest, grid_rows):
    b_refs = rest[:C_STEP_TILES]
    z_ref, g_ref, y_ref = rest[C_STEP_TILES:]
    for h in range(C_STEP_TILES):
        r0 = (pl.program_id(1) * C_STEP_TILES + h) * C_ROWS
        ws = jnp.clip(r0 - NA_ROWS // 2, 0, grid_rows - C_WIN)
        ks = pl.multiple_of(ws * GRID_W, GRID_W)
        rows = slice(h * C_TQ, (h + 1) * C_TQ)
        tiles = []
        for p in range(C_W // LANES):
            cols = slice(p * LANES, (p + 1) * LANES)
            q2 = _stack_heads(q_ref[rows, cols])
            s = _dot_nt(q2, k_ref[pl.ds(ks, C_NK), cols]) + b_refs[h][p]
            m = jnp.max(s, axis=1, keepdims=True)
            e = jnp.exp(s - m)
            l = jnp.sum(e, axis=1, keepdims=True)
            o2 = jnp.dot(e.astype(_BF16), v_ref[pl.ds(ks, C_NK), cols], preferred_element_type=_F32) / l
            tiles.append(_unstack_heads(o2))
        yc = jnp.concatenate(tiles, axis=1)
        y_ref[rows, :] = _gated_norm(yc, z_ref[rows, :], g_ref[...]).astype(_BF16)


def _attn_c(cqkv, bias, cz, gain, batch, seq):
    T = cqkv.shape[0]
    grid_rows = seq // GRID_W
    n_tiles = seq // C_TQ
    assert grid_rows >= C_WIN and n_tiles % C_STEP_TILES == 0
    nq = n_tiles // C_STEP_TILES
    tq = C_STEP_TILES * C_TQ
    n_pair = C_W // LANES
    bias = bias.reshape(3, n_pair, 2 * C_TQ, C_NK)

    def bias_spec(h):
        def index(b, t):
            tile = t * C_STEP_TILES + h
            return (jnp.where(tile == 0, 0, jnp.where(tile == n_tiles - 1, 2, 1)), 0, 0, 0)
        return pl.BlockSpec((None, n_pair, 2 * C_TQ, C_NK), index)

    kern = functools.partial(_attn_c_kernel, grid_rows=grid_rows)
    tok = lambda b, t: (b * nq + t, 0)
    resident = lambda col: pl.BlockSpec((seq, C_W), lambda b, t: (b, col), pipeline_mode=pl.Buffered(1))
    return pl.pallas_call(
        kern,
        grid=(batch, nq),
        in_specs=[pl.BlockSpec((tq, C_W), tok), resident(1), resident(2)]
        + [bias_spec(h) for h in range(C_STEP_TILES)]
        + [pl.BlockSpec((tq, C_W), tok), pl.BlockSpec((1, C_W), lambda b, t: (0, 0))],
        out_specs=pl.BlockSpec((tq, C_W), tok),
        out_shape=jax.ShapeDtypeStruct((T, C_W), _BF16),
        compiler_params=_cparams(2),
        name="attn_c",
    )(cqkv, cqkv, cqkv, *([bias] * C_STEP_TILES), cz, gain)


def _outproj_kernel(ya_ref, yb_ref, yc_ref, w_ref, npost_ref, x_ref, out_ref):
    y = jnp.concatenate([ya_ref[...], yb_ref[...], yc_ref[...]], axis=1)
    t = jnp.dot(y, w_ref[...], preferred_element_type=_F32)
    r = lax.rsqrt(jnp.mean(t * t, axis=-1, keepdims=True) + EPS)
    out_ref[...] = x_ref[...] + t * r * npost_ref[...]


def _outproj(ya, yb, yc, w, npost, x2d):
    T = x2d.shape[0]
    tm = min(ROW_TILE, T)
    row = lambda i: (i, 0)
    const = lambda i: (0, 0)
    return pl.pallas_call(
        _outproj_kernel,
        grid=(T // tm,),
        in_specs=[pl.BlockSpec((tm, A_W), row), pl.BlockSpec((tm, B_W), row), pl.BlockSpec((tm, C_W), row),
                  pl.BlockSpec((MIX_W, D_MODEL), const, pipeline_mode=pl.Buffered(1)),
                  pl.BlockSpec((1, D_MODEL), const), pl.BlockSpec((tm, D_MODEL), row)],
        out_specs=pl.BlockSpec((tm, D_MODEL), row),
        out_shape=jax.ShapeDtypeStruct((T, D_MODEL), _F32),
        compiler_params=_cparams(1),
        name="outproj",
    )(ya, yb, yc, w, npost, x2d)


def _pair_table(per_head):
    return jnp.concatenate([per_head, per_head], axis=1)


def _rope_tables(seq):
    n = ROPE_DIMS // 2
    freqs = ROPE_THETA ** (-jnp.arange(0, ROPE_DIMS, 2, dtype=_F32) / ROPE_DIMS)
    ang = jnp.arange(seq, dtype=_F32)[:, None] * freqs
    cos, sin = jnp.cos(ang), jnp.sin(ang)
    ones = jnp.ones((seq, HEAD_DIM - ROPE_DIMS), _F32)
    zeros = jnp.zeros((seq, HEAD_DIM - ROPE_DIMS), _F32)
    zn = jnp.zeros((seq, n), _F32)
    c = jnp.concatenate([cos, cos, ones], axis=1)
    s_up = jnp.concatenate([-sin, zn, zeros], axis=1)
    s_dn = jnp.concatenate([zn, sin, zeros], axis=1)
    return jnp.stack([_pair_table(c), _pair_table(s_up), _pair_table(s_dn)])


def _axial_tables(seq):
    half = HEAD_DIM // 2
    freqs = AXIAL_THETA ** (-jnp.arange(0, half, 2, dtype=_F32) / half)
    t = jnp.arange(seq)
    ang_row = (t // GRID_W).astype(_F32)[:, None] * freqs
    ang_col = (t % GRID_W).astype(_F32)[:, None] * freqs
    zn = jnp.zeros((seq, half // 2), _F32)
    parts_c, parts_up, parts_dn = [], [], []
    for ang in (ang_row, ang_col):
        cos, sin = jnp.cos(ang), jnp.sin(ang)
        parts_c += [cos, cos]
        parts_up += [-sin, zn]
        parts_dn += [zn, sin]
    cat = lambda parts: _pair_table(jnp.concatenate(parts, axis=1))
    return jnp.stack([cat(parts_c), cat(parts_up), cat(parts_dn)])


def _pack_w_in(w_in):
    cuts = {}
    off = 0
    for name, width in (("aq", A_W), ("ak", A_W), ("av", A_W), ("az", A_W), ("bq", B_W), ("bk", B_KV_W),
                        ("bv", B_KV_W), ("bz", B_W), ("cq", C_W), ("ck", C_W), ("cv", C_W), ("cz", C_W)):
        cuts[name] = w_in[..., off:off + width]
        off += width

    def dup(w):
        lead = w.shape[:-1]
        w = w.reshape(lead + (B_KV_HEADS, 1, HEAD_DIM))
        return jnp.broadcast_to(w, lead + (B_KV_HEADS, 2, HEAD_DIM)).reshape(lead + (2 * B_KV_W,))

    order = [cuts["aq"], cuts["ak"], cuts["av"], cuts["cq"], cuts["ck"], cuts["cv"], cuts["bq"],
             dup(cuts["bk"]), dup(cuts["bv"]), cuts["az"], cuts["bz"], cuts["cz"]]
    return jnp.concatenate(order, axis=-1).astype(_BF16)


def _layer(x2d, batch, seq, tabs, norm_pre, w_in, gq, gk, bd, bias, gains, w_out, npost):
    tab_a, tab_b = tabs
    ga, gb, gc = gains
    a1, a4, a16, cqkv, bq, bkv, az, bz, cz = _inproj(x2d, seq, norm_pre, w_in, tab_a, tab_b, gq, gk, bd)
    radii = {dil: window // (2 * dil) for window, dil in A_PATTERNS}
    o4, l4 = _attn_a_dil(a4, batch, seq, 4, radii[4])
    o16, l16 = _attn_a_dil(a16, batch, seq, 16, radii[16])
    ya = _attn_a_final(a1, o4, l4, o16, l16, az, ga, batch, seq, radii[1])
    yb = _attn_b(bq, bkv, bz, gb, batch, seq)
    yc = _attn_c(cqkv, bias, cz, gc, batch, seq)
    return _outproj(ya, yb, yc, w_out, npost, x2d)


def kernel(x_prompt, x_sample, norm_pre, w_in, q_norm, k_norm, rel_bias, branch_gain, w_out, norm_post):
    depth = w_in.shape[0]
    w_in_p = _pack_w_in(w_in)
    w_out_p = w_out.astype(_BF16)
    bd = jnp.kron(jnp.eye(LANES // HEAD_DIM, dtype=_F32),
                  jnp.full((HEAD_DIM, HEAD_DIM), 1.0 / HEAD_DIM, _F32)).astype(_BF16)
    groups = []
    for x in (x_prompt, x_sample):
        batch, seq, _ = x.shape
        groups.append(dict(y=x.reshape(batch * seq, D_MODEL), batch=batch, seq=seq,
                           tabs=(_rope_tables(seq), _axial_tables(seq))))
    for l in range(depth):
        bias = _bias_table(rel_bias[l])
        gq = _pair_table(q_norm[l][None, :])
        gk = _pair_table(k_norm[l][None, :])
        gain = branch_gain[l][None, :]
        gains = (gain[:, 0:A_W], gain[:, A_W:A_W + B_W], gain[:, A_W + B_W:MIX_W])
        for g in groups:
            g["y"] = _layer(g["y"], g["batch"], g["seq"], g["tabs"], norm_pre[l][None, :], w_in_p[l],
                            gq, gk, bd, bias, gains, w_out_p[l], norm_post[l][None, :])
    return tuple(g["y"].reshape(g["batch"], g["seq"], D_MODEL) for g in groups)
```

```python
import functools
import math

import jax
import jax.numpy as jnp
import numpy as np
from jax import lax
from jax.experimental import pallas as pl
from jax.experimental.pallas import tpu as pltpu

D_MODEL = 1024
HEAD_DIM = 64
GRID_W = 64
EPS = 1e-6
A_HEADS = 6
A_PATTERNS = ((128, 1), (512, 4), (2048, 16))
ROPE_THETA = 500000.0
ROPE_DIMS = HEAD_DIM // 4
B_HEADS = 4
B_KV_HEADS = 2
AXIAL_THETA = 10000.0
C_HEADS = 6
NA_ROWS = 8
NA_COLS = 16

A_W = A_HEADS * HEAD_DIM
B_W = B_HEADS * HEAD_DIM
B_KV_W = B_KV_HEADS * HEAD_DIM
C_W = C_HEADS * HEAD_DIM
MIX_W = A_W + B_W + C_W
QKV_W = 3 * A_W
OFF_A = 0
OFF_AZ = OFF_A + QKV_W
OFF_B = OFF_AZ + A_W
OFF_BZ = OFF_B + B_W + 2 * B_KV_W
OFF_C = OFF_BZ + B_W
OFF_CZ = OFF_C + QKV_W

LANES = 128
A_TILES = A_W // LANES
Q_SCALE = HEAD_DIM ** -0.5
LOG2E = math.log2(math.e)
VMEM_LIMIT = 56 * 1024 * 1024

ROW_TILE = 512
A_SUB = 128
A_STEP_TOKENS = 2048
A1_Q_BLOCK = 512
B_TQ = 512
B_TK = 2048
C_ROWS = 4
C_WIN = 12
C_TQ = C_ROWS * GRID_W
C_STEP_TILES = 2
C_NK = C_WIN * GRID_W
NEG_INF = float("-inf")

_F32 = jnp.float32
_BF16 = jnp.bfloat16


def _cparams(n_grid):
    return pltpu.CompilerParams(dimension_semantics=("arbitrary",) * n_grid,
                                vmem_limit_bytes=VMEM_LIMIT)


def _lane_lo(shape):
    return lax.broadcasted_iota(jnp.int32, shape, len(shape) - 1) % LANES < HEAD_DIM


def _stack_heads(q):
    lo = _lane_lo(q.shape)
    zero = jnp.zeros_like(q)
    return jnp.concatenate([jnp.where(lo, q, zero), jnp.where(lo, zero, q)], axis=0)


def _unstack_heads(o2):
    n = o2.shape[0] // 2
    return jnp.where(_lane_lo((n, o2.shape[1])), o2[:n], o2[n:])


def _dot_nt(a, b):
    return lax.dot_general(a, b, (((1,), (1,)), ((), ())), preferred_element_type=_F32)


def _gated_norm(y, z, gain):
    z = z.astype(_F32)
    u = y * (z * jax.nn.sigmoid(z))
    r = lax.rsqrt(jnp.mean(u * u, axis=-1, keepdims=True) + EPS)
    return u * r * gain


def _rotate_tile(t, tab_ref, shift):
    up = pltpu.roll(t, LANES - shift, axis=1)
    dn = pltpu.roll(t, shift, axis=1)
    return t * tab_ref[0] + up * tab_ref[1] + dn * tab_ref[2]


def _inproj_kernel(x_ref, g_ref, w_ref, ta_ref, tb_ref, gq_ref, gk_ref, bd_ref,
                   a1_ref, a4_ref, a16_ref, c_ref, bq_ref, bkv_ref, az_ref, bz_ref, cz_ref, scr_ref):
    x = x_ref[...]
    tm = x.shape[0]
    r = lax.rsqrt(jnp.mean(x * x, axis=-1, keepdims=True) + EPS)
    h = (x * r * g_ref[...]).astype(_BF16)
    p = jnp.dot(h, w_ref[...], preferred_element_type=_F32)

    for j in range(QKV_W // LANES):
        t = p[:, OFF_A + j * LANES:OFF_A + (j + 1) * LANES]
        if j < 2 * A_TILES:
            t = _rotate_tile(t, ta_ref, ROPE_DIMS // 2)
        if j < A_TILES:
            t = t * (Q_SCALE * LOG2E)
        a1_ref[:, j * LANES:(j + 1) * LANES] = t.astype(_BF16)
        scr_ref[j] = t
    for dil, ref in ((4, a4_ref), (16, a16_ref)):
        for res in range(dil):
            for j in range(QKV_W // LANES):
                rows = scr_ref[j, pl.ds(res, tm // dil, stride=dil), :]
                ref[res, :, j * LANES:(j + 1) * LANES] = rows.astype(_BF16)

    c_ref[:, 0:C_W] = (p[:, OFF_C:OFF_C + C_W] * (Q_SCALE * LOG2E)).astype(_BF16)
    c_ref[:, C_W:QKV_W] = p[:, OFF_C + C_W:OFF_C + QKV_W].astype(_BF16)

    bd = bd_ref[...]

    def headnorm(t, gain):
        width = t.shape[1]
        sq = t * t
        hi = sq.astype(_BF16)
        lo = (sq - hi.astype(_F32)).astype(_BF16)
        ms = (jnp.dot(hi, bd[:width, :width], preferred_element_type=_F32)
              + jnp.dot(lo, bd[:width, :width], preferred_element_type=_F32))
        return t * lax.rsqrt(ms + EPS) * gain

    qn = headnorm(p[:, OFF_B:OFF_B + B_W], jnp.concatenate([gq_ref[...]] * (B_W // LANES), axis=1))
    for j in range(B_W // LANES):
        t = _rotate_tile(qn[:, j * LANES:(j + 1) * LANES], tb_ref, HEAD_DIM // 4) * (Q_SCALE * LOG2E)
        bq_ref[:, j * LANES:(j + 1) * LANES] = t.astype(_BF16)
    lo_half = _lane_lo((tm, LANES))
    kn = _rotate_tile(headnorm(p[:, OFF_B + B_W:OFF_B + B_W + B_KV_W], gk_ref[...]), tb_ref, HEAD_DIM // 4)
    k_sw = pltpu.roll(kn, HEAD_DIM, axis=1)
    bkv_ref[:, 0:LANES] = jnp.where(lo_half, kn, k_sw).astype(_BF16)
    bkv_ref[:, LANES:2 * LANES] = jnp.where(lo_half, k_sw, kn).astype(_BF16)
    vv = p[:, OFF_B + B_W + B_KV_W:OFF_B + B_W + 2 * B_KV_W]
    bkv_ref[:, B_W:B_W + LANES] = jnp.where(lo_half, vv, 1.0).astype(_BF16)
    bkv_ref[:, B_W + LANES:2 * B_W] = jnp.where(lo_half, pltpu.roll(vv, HEAD_DIM, axis=1), 1.0).astype(_BF16)

    az_ref[...] = p[:, OFF_AZ:OFF_AZ + A_W].astype(_BF16)
    bz_ref[...] = p[:, OFF_BZ:OFF_BZ + B_W].astype(_BF16)
    cz_ref[...] = p[:, OFF_CZ:OFF_CZ + C_W].astype(_BF16)


def _inproj(x2d, seq, norm_pre, w, tab_a, tab_b, gq, gk, bd):
    T = x2d.shape[0]
    tm = min(ROW_TILE, seq)
    n_seq = seq // tm
    nw = w.shape[1]
    const = lambda i: (0, 0)
    row = lambda i: (i, 0)
    row3 = lambda i: (0, i, 0)
    tab = lambda i: (0, i % n_seq, 0)
    out_shape = (
        jax.ShapeDtypeStruct((T, QKV_W), _BF16),
        jax.ShapeDtypeStruct((4, T // 4, QKV_W), _BF16),
        jax.ShapeDtypeStruct((16, T // 16, QKV_W), _BF16),
        jax.ShapeDtypeStruct((T, QKV_W), _BF16),
        jax.ShapeDtypeStruct((T, B_W), _BF16),
        jax.ShapeDtypeStruct((T, 2 * B_W), _BF16),
        jax.ShapeDtypeStruct((T, A_W), _BF16),
        jax.ShapeDtypeStruct((T, B_W), _BF16),
        jax.ShapeDtypeStruct((T, C_W), _BF16),
    )
    out_specs = (
        pl.BlockSpec((tm, QKV_W), row),
        pl.BlockSpec((4, tm // 4, QKV_W), row3),
        pl.BlockSpec((16, tm // 16, QKV_W), row3),
        pl.BlockSpec((tm, QKV_W), row),
        pl.BlockSpec((tm, B_W), row),
        pl.BlockSpec((tm, 2 * B_W), row),
        pl.BlockSpec((tm, A_W), row),
        pl.BlockSpec((tm, B_W), row),
        pl.BlockSpec((tm, C_W), row),
    )
    in_specs = [
        pl.BlockSpec((tm, D_MODEL), row),
        pl.BlockSpec((1, D_MODEL), const),
        pl.BlockSpec((D_MODEL, nw), const, pipeline_mode=pl.Buffered(1)),
        pl.BlockSpec((3, tm, LANES), tab),
        pl.BlockSpec((3, tm, LANES), tab),
        pl.BlockSpec((1, LANES), const),
        pl.BlockSpec((1, LANES), const),
        pl.BlockSpec((B_W, B_W), const),
    ]
    return pl.pallas_call(
        _inproj_kernel,
        grid=(T // tm,),
        in_specs=in_specs,
        out_specs=out_specs,
        out_shape=out_shape,
        scratch_shapes=[pltpu.VMEM((QKV_W // LANES, tm, LANES), _F32)],
        compiler_params=_cparams(1),
        name="inproj",
    )(x2d, norm_pre, w, tab_a, tab_b, gq, gk, bd)


def _band_tile(q, k_ref, v_ref, kidx, kstart, nk, cols, bias):
    q2 = _stack_heads(q)
    kp = k_ref[kidx + (pl.ds(kstart, nk), cols)]
    vp = v_ref[kidx + (pl.ds(kstart, nk), cols)]
    s = _dot_nt(q2, kp) + bias
    m = jnp.max(s, axis=1, keepdims=True)
    e = jnp.exp2(s - m).astype(_BF16)
    ol = jnp.dot(e, jnp.concatenate([vp, jnp.ones_like(vp)], axis=1), preferred_element_type=_F32)
    l = ol[:, LANES:]
    return _unstack_heads(ol[:, :LANES] / l), _unstack_heads(m + jnp.log2(l))


def _band_delta(sub, nk):
    row = lax.broadcasted_iota(jnp.int32, (2 * sub, nk), 0) % sub
    col = lax.broadcasted_iota(jnp.int32, (2 * sub, nk), 1)
    return col - row


def _band_bias(delta, shift, radius):
    rel = delta + shift
    return jnp.where((rel >= -radius) & (rel <= radius), 0.0, NEG_INF).astype(_F32)


def _attn_a_dil_kernel(q_ref, k_ref, v_ref, o_ref, lse_ref, *, dil, q_block, seq_len, radius, sub):
    nk = min(sub + 2 * radius, seq_len)
    base = pl.program_id(1) * q_block
    delta = _band_delta(sub, nk)
    windows = []
    for si in range(q_block // sub):
        l0 = base + si * sub
        kstart = pl.multiple_of(jnp.clip(l0 - radius, 0, seq_len - nk), HEAD_DIM)
        windows.append((kstart, _band_bias(delta, kstart - l0, radius)))

    def body(res, carry):
        for si, (kstart, bias) in enumerate(windows):
            for p in range(A_TILES):
                cols = slice(p * LANES, (p + 1) * LANES)
                q = q_ref[res, si * sub:(si + 1) * sub, cols]
                o, lse = _band_tile(q, k_ref, v_ref, (res,), kstart, nk, cols, bias)
                rows = pl.ds(si * sub * dil + res, sub, stride=dil)
                o_ref[p, rows, :] = o
                lse_ref[p, rows, :] = lse
        return carry

    lax.fori_loop(0, dil, body, 0, unroll=max(1, min(dil, 8 * sub // q_block)))


def _attn_a_dil(a_res, batch, seq, dil, radius):
    L = seq // dil
    tl = min(max(A_STEP_TOKENS // dil, A_SUB), L)
    sub = min(A_SUB, tl)
    n_lt = L // tl
    T = batch * seq
    kern = functools.partial(_attn_a_dil_kernel, dil=dil, q_block=tl, seq_len=L, radius=radius, sub=sub)
    q_spec = pl.BlockSpec((dil, tl, A_W), lambda b, t: (0, b * n_lt + t, 0))
    k_spec = pl.BlockSpec((dil, L, A_W), lambda b, t: (0, b, 1))
    v_spec = pl.BlockSpec((dil, L, A_W), lambda b, t: (0, b, 2))
    o_spec = pl.BlockSpec((A_TILES, tl * dil, LANES), lambda b, t: (0, b * n_lt + t, 0))
    out_shape = (jax.ShapeDtypeStruct((A_TILES, T, LANES), _F32),) * 2
    return pl.pallas_call(
        kern,
        grid=(batch, n_lt),
        in_specs=[q_spec, k_spec, v_spec],
        out_specs=(o_spec, o_spec),
        out_shape=out_shape,
        compiler_params=_cparams(2),
        name=f"attn_a_d{dil}",
    )(a_res, a_res, a_res)


def _attn_a_final_kernel(q_ref, k_ref, v_ref, o4_ref, l4_ref, o16_ref, l16_ref, z_ref, g_ref, y_ref,
                         *, q_block, seq_len, radius, sub):
    nk = min(sub + 2 * radius, seq_len)
    n_sub = q_block // sub
    base = pl.program_id(1) * q_block
    delta = _band_delta(sub, nk)
    inner_bias = _band_bias(delta, -radius, radius)
    for si in range(n_sub):
        l0 = base + si * sub
        kstart = pl.multiple_of(jnp.clip(l0 - radius, 0, seq_len - nk), HEAD_DIM)
        inner = nk == sub + 2 * radius and 0 < si < n_sub - 1
        bias = inner_bias if inner else _band_bias(delta, kstart - l0, radius)
        rows = slice(si * sub, (si + 1) * sub)
        tiles = []
        for p in range(A_TILES):
            cols = slice(p * LANES, (p + 1) * LANES)
            o1, l1 = _band_tile(q_ref[rows, cols], k_ref, v_ref, (), kstart, nk, cols, bias)
            l4, l16 = l4_ref[p, rows, :], l16_ref[p, rows, :]
            m = jnp.maximum(jnp.maximum(l1, l4), l16)
            e1, e4, e16 = jnp.exp2(l1 - m), jnp.exp2(l4 - m), jnp.exp2(l16 - m)
            mix = e1 * o1 + e4 * o4_ref[p, rows, :] + e16 * o16_ref[p, rows, :]
            tiles.append(mix / (e1 + e4 + e16))
        ya = jnp.concatenate(tiles, axis=1)
        y_ref[rows, :] = _gated_norm(ya, z_ref[rows, :], g_ref[...]).astype(_BF16)


def _attn_a_final(a1, o4, l4, o16, l16, az, gain, batch, seq, radius):
    T = a1.shape[0]
    tl = min(A1_Q_BLOCK, seq)
    sub = min(A_SUB, tl)
    n_lt = seq // tl
    kern = functools.partial(_attn_a_final_kernel, q_block=tl, seq_len=seq, radius=radius, sub=sub)
    tok = lambda b, t: (b * n_lt + t, 0)
    tok3 = lambda b, t: (0, b * n_lt + t, 0)
    st_spec = pl.BlockSpec((A_TILES, tl, LANES), tok3)
    return pl.pallas_call(
        kern,
        grid=(batch, n_lt),
        in_specs=[
            pl.BlockSpec((tl, A_W), tok),
            pl.BlockSpec((seq, A_W), lambda b, t: (b, 1)),
            pl.BlockSpec((seq, A_W), lambda b, t: (b, 2)),
            st_spec, st_spec, st_spec, st_spec,
            pl.BlockSpec((tl, A_W), tok),
            pl.BlockSpec((1, A_W), lambda b, t: (0, 0)),
        ],
        out_specs=pl.BlockSpec((tl, A_W), tok),
        out_shape=jax.ShapeDtypeStruct((T, A_W), _BF16),
        compiler_params=_cparams(2),
        name="attn_a_d1",
    )(a1, a1, a1, o4, l4, o16, l16, az, gain)


def _attn_b_kernel(q_ref, kv_ref, z_ref, g_ref, y_ref, *, seq, tk):
    n_grp = B_W // LANES
    q2s = [_stack_heads(q_ref[:, g * LANES:(g + 1) * LANES]) for g in range(n_grp)]
    n2 = q2s[0].shape[0]

    def body(kt, carry):
        ks = pl.multiple_of(kt * tk, tk)
        out = []
        for g in range(n_grp):
            m, acc = carry[g]
            k = kv_ref[pl.ds(ks, tk), g * LANES:(g + 1) * LANES]
            v = kv_ref[pl.ds(ks, tk), B_W + g * LANES:B_W + (g + 1) * LANES]
            s = _dot_nt(q2s[g], k)
            m_new = jnp.maximum(m, jnp.max(s, axis=1, keepdims=True))
            alpha = jnp.exp2(m - m_new)
            p = jnp.exp2(s - m_new)
            acc = alpha * acc + jnp.dot(p.astype(_BF16), v, preferred_element_type=_F32)
            out.append((m_new, acc))
        return tuple(out)

    init = tuple((jnp.full((n2, 1), NEG_INF, _F32), jnp.zeros((n2, LANES), _F32)) for _ in range(n_grp))
    final = lax.fori_loop(0, seq // tk, body, init)
    tq = n2 // 2
    tiles = []
    for g in range(n_grp):
        acc = final[g][1]
        on = acc / pltpu.roll(acc, HEAD_DIM, axis=1)
        tiles.append(jnp.where(_lane_lo((tq, LANES)), on[:tq], pltpu.roll(on[tq:], HEAD_DIM, axis=1)))
    yb = jnp.concatenate(tiles, axis=1)
    y_ref[...] = _gated_norm(yb, z_ref[...], g_ref[...]).astype(_BF16)


def _attn_b(bq, bkv, bz, gain, batch, seq):
    T = bq.shape[0]
    tq = min(B_TQ, seq)
    tk = min(B_TK, seq)
    nq = seq // tq
    kern = functools.partial(_attn_b_kernel, seq=seq, tk=tk)
    tok = lambda b, t: (b * nq + t, 0)
    return pl.pallas_call(
        kern,
        grid=(batch, nq),
        in_specs=[
            pl.BlockSpec((tq, B_W), tok),
            pl.BlockSpec((seq, 2 * B_W), lambda b, t: (b, 0)),
            pl.BlockSpec((tq, B_W), tok),
            pl.BlockSpec((1, B_W), lambda b, t: (0, 0)),
        ],
        out_specs=pl.BlockSpec((tq, B_W), tok),
        out_shape=jax.ShapeDtypeStruct((T, B_W), _BF16),
        compiler_params=_cparams(2),
        name="attn_b",
    )(bq, bkv, bz, gain)


def _c_tile_types():
    half = NA_ROWS // 2
    first = dict(ro_base=NA_ROWS - 1, start=[0] * C_ROWS)
    inner = dict(ro_base=NA_ROWS - 1 - half, start=list(range(C_ROWS)))
    last = dict(ro_base=NA_ROWS - 1 - (C_WIN - C_ROWS), start=[C_WIN - NA_ROWS] * C_ROWS)
    return (first, inner, last)


def _bias_kernel(rb_ref, out_ref):
    h = pl.program_id(0)
    n_ro = 2 * NA_ROWS - 1
    n_co = 2 * NA_COLS - 1
    qi = lax.broadcasted_iota(jnp.int32, (GRID_W, LANES), 0)
    lane = lax.broadcasted_iota(jnp.int32, (GRID_W, LANES), 1)
    kc = lane % GRID_W
    d = kc - qi
    cs = jnp.clip(qi - NA_COLS // 2, 0, GRID_W - NA_COLS)
    col_valid = (kc >= cs) & (kc < cs + NA_COLS)
    hits = [d == co - (NA_COLS - 1) for co in range(n_co)]
    neg = jnp.full((GRID_W, LANES), NEG_INF, _F32)
    rows = []
    for ro in range(n_ro):
        e = neg
        for co in range(n_co):
            e = jnp.where(hits[co], rb_ref[(h * n_ro + ro) * n_co + co] * LOG2E, e)
        rows.append(jnp.where(col_valid, e, neg))
    lo = lane < GRID_W
    for t, spec in enumerate(_c_tile_types()):
        for j in range(C_ROWS):
            def blk(kk):
                ok = spec["start"][j] <= kk < spec["start"][j] + NA_ROWS
                return rows[kk - j + spec["ro_base"]] if ok else neg
            tiles = [jnp.where(lo, blk(2 * i), blk(2 * i + 1)) for i in range(C_WIN // 2)]
            out_ref[t, j * GRID_W:(j + 1) * GRID_W, :] = jnp.concatenate(tiles, axis=1)


def _bias_table(rel_bias):
    flat = rel_bias.reshape(-1)
    return pl.pallas_call(
        _bias_kernel,
        grid=(C_HEADS,),
        in_specs=[pl.BlockSpec(memory_space=pltpu.SMEM)],
        out_specs=pl.BlockSpec((3, None, C_TQ, C_NK), lambda h: (0, h, 0, 0)),
        out_shape=jax.ShapeDtypeStruct((3, C_HEADS, C_TQ, C_NK), _F32),
        compiler_params=_cparams(1),
        name="c_bias",
    )(flat)


def _attn_c_kernel(q_ref, k_ref, v_ref, *rest, grid_rows):
    b_refs = rest[:C_STEP_TILES]
    z_ref, g_ref, y_ref = rest[C_STEP_TILES:]
    for h in range(C_STEP_TILES):
        r0 = (pl.program_id(1) * C_STEP_TILES + h) * C_ROWS
        ws = jnp.clip(r0 - NA_ROWS // 2, 0, grid_rows - C_WIN)
        ks = pl.multiple_of(ws * GRID_W, GRID_W)
        rows = slice(h * C_TQ, (h + 1) * C_TQ)
        tiles = []
        for p in range(C_W // LANES):
            cols = slice(p * LANES, (p + 1) * LANES)
            q2 = _stack_heads(q_ref[rows, cols])
            s = _dot_nt(q2, k_ref[pl.ds(ks, C_NK), cols]) + b_refs[h][p]
            m = jnp.max(s, axis=1, keepdims=True)
            e = jnp.exp2(s - m).astype(_BF16)
            vp = v_ref[pl.ds(ks, C_NK), cols]
            ol = jnp.dot(e, jnp.concatenate([vp, jnp.ones_like(vp)], axis=1), preferred_element_type=_F32)
            tiles.append(_unstack_heads(ol[:, :LANES] / ol[:, LANES:]))
        yc = jnp.concatenate(tiles, axis=1)
        y_ref[rows, :] = _gated_norm(yc, z_ref[rows, :], g_ref[...]).astype(_BF16)


def _attn_c(cqkv, bias, cz, gain, batch, seq):
    T = cqkv.shape[0]
    grid_rows = seq // GRID_W
    n_tiles = seq // C_TQ
    assert grid_rows >= C_WIN and n_tiles % C_STEP_TILES == 0
    nq = n_tiles // C_STEP_TILES
    tq = C_STEP_TILES * C_TQ
    n_pair = C_W // LANES
    bias = bias.reshape(3, n_pair, 2 * C_TQ, C_NK)

    def bias_spec(h):
        def index(b, t):
            tile = t * C_STEP_TILES + h
            return (jnp.where(tile == 0, 0, jnp.where(tile == n_tiles - 1, 2, 1)), 0, 0, 0)
        return pl.BlockSpec((None, n_pair, 2 * C_TQ, C_NK), index)

    kern = functools.partial(_attn_c_kernel, grid_rows=grid_rows)
    tok = lambda b, t: (b * nq + t, 0)
    resident = lambda col: pl.BlockSpec((seq, C_W), lambda b, t: (b, col), pipeline_mode=pl.Buffered(1))
    return pl.pallas_call(
        kern,
        grid=(batch, nq),
        in_specs=[pl.BlockSpec((tq, C_W), tok), resident(1), resident(2)]
        + [bias_spec(h) for h in range(C_STEP_TILES)]
        + [pl.BlockSpec((tq, C_W), tok), pl.BlockSpec((1, C_W), lambda b, t: (0, 0))],
        out_specs=pl.BlockSpec((tq, C_W), tok),
        out_shape=jax.ShapeDtypeStruct((T, C_W), _BF16),
        compiler_params=_cparams(2),
        name="attn_c",
    )(cqkv, cqkv, cqkv, *([bias] * C_STEP_TILES), cz, gain)


def _outproj_kernel(ya_ref, yb_ref, yc_ref, w_ref, npost_ref, x_ref, out_ref):
    y = jnp.concatenate([ya_ref[...], yb_ref[...], yc_ref[...]], axis=1)
    t = jnp.dot(y, w_ref[...], preferred_element_type=_F32)
    r = lax.rsqrt(jnp.mean(t * t, axis=-1, keepdims=True) + EPS)
    out_ref[...] = x_ref[...] + t * r * npost_ref[...]


def _outproj(ya, yb, yc, w, npost, x2d):
    T = x2d.shape[0]
    tm = min(ROW_TILE, T)
    row = lambda i: (i, 0)
    const = lambda i: (0, 0)
    return pl.pallas_call(
        _outproj_kernel,
        grid=(T // tm,),
        in_specs=[pl.BlockSpec((tm, A_W), row), pl.BlockSpec((tm, B_W), row), pl.BlockSpec((tm, C_W), row),
                  pl.BlockSpec((MIX_W, D_MODEL), const, pipeline_mode=pl.Buffered(1)),
                  pl.BlockSpec((1, D_MODEL), const), pl.BlockSpec((tm, D_MODEL), row)],
        out_specs=pl.BlockSpec((tm, D_MODEL), row),
        out_shape=jax.ShapeDtypeStruct((T, D_MODEL), _F32),
        compiler_params=_cparams(1),
        name="outproj",
    )(ya, yb, yc, w, npost, x2d)


def _lane_freqs(freqs, n_rot, width):
    lane = np.arange(LANES) % width
    idx = lane % (n_rot // 2)
    rot = lane < n_rot
    first = rot & (lane % n_rot < n_rot // 2)
    second = rot & ~first
    return jnp.where(rot, freqs[idx], 0.0), first, second


def _rotary_tables(ang, first, second):
    cos, sin = jnp.cos(ang), jnp.sin(ang)
    zero = jnp.zeros_like(sin)
    return jnp.stack([cos, jnp.where(first, -sin, zero), jnp.where(second, sin, zero)])


def _rope_tables(seq):
    freqs = ROPE_THETA ** (-jnp.arange(0, ROPE_DIMS, 2, dtype=_F32) / ROPE_DIMS)
    lane_f, first, second = _lane_freqs(freqs, ROPE_DIMS, HEAD_DIM)
    ang = jnp.arange(seq, dtype=_F32)[:, None] * lane_f[None, :]
    return _rotary_tables(ang, first, second)


def _axial_tables(seq):
    half = HEAD_DIM // 2
    freqs = AXIAL_THETA ** (-jnp.arange(0, half, 2, dtype=_F32) / half)
    lane_f, first, second = _lane_freqs(freqs, half, half)
    t = jnp.arange(seq)
    by_row = (np.arange(LANES) % HEAD_DIM) < half
    pos = jnp.where(by_row[None, :], (t // GRID_W).astype(_F32)[:, None], (t % GRID_W).astype(_F32)[:, None])
    return _rotary_tables(pos * lane_f[None, :], first, second)


def _layer(x2d, batch, seq, tabs, norm_pre, w_in, gq, gk, bd, bias, gains, w_out, npost):
    tab_a, tab_b = tabs
    ga, gb, gc = gains
    a1, a4, a16, cqkv, bq, bkv, az, bz, cz = _inproj(x2d, seq, norm_pre, w_in, tab_a, tab_b, gq, gk, bd)
    radii = {dil: window // (2 * dil) for window, dil in A_PATTERNS}
    o4, l4 = _attn_a_dil(a4, batch, seq, 4, radii[4])
    o16, l16 = _attn_a_dil(a16, batch, seq, 16, radii[16])
    ya = _attn_a_final(a1, o4, l4, o16, l16, az, ga, batch, seq, radii[1])
    yb = _attn_b(bq, bkv, bz, gb, batch, seq)
    yc = _attn_c(cqkv, bias, cz, gc, batch, seq)
    return _outproj(ya, yb, yc, w_out, npost, x2d)


def kernel(x_prompt, x_sample, norm_pre, w_in, q_norm, k_norm, rel_bias, branch_gain, w_out, norm_post):
    depth = w_in.shape[0]
    w_in_p = w_in.astype(_BF16)
    w_out_p = w_out.astype(_BF16)
    bd = jnp.asarray(np.kron(np.eye(B_W // HEAD_DIM), np.full((HEAD_DIM, HEAD_DIM), 1.0 / HEAD_DIM)), _BF16)
    groups = []
    for x in (x_prompt, x_sample):
        batch, seq, _ = x.shape
        groups.append(dict(y=x.reshape(batch * seq, D_MODEL), batch=batch, seq=seq,
                           tabs=(_rope_tables(seq), _axial_tables(seq))))
    for l in range(depth):
        bias = _bias_table(rel_bias[l])
        gq = jnp.tile(q_norm[l][None, :], (1, LANES // HEAD_DIM))
        gk = jnp.tile(k_norm[l][None, :], (1, LANES // HEAD_DIM))
        gain = branch_gain[l][None, :]
        gains = (gain[:, 0:A_W], gain[:, A_W:A_W + B_W], gain[:, A_W + B_W:MIX_W])
        for g in groups:
            g["y"] = _layer(g["y"], g["batch"], g["seq"], g["tabs"], norm_pre[l][None, :], w_in_p[l],
                            gq, gk, bd, bias, gains, w_out_p[l], norm_post[l][None, :])
    return tuple(g["y"].reshape(g["batch"], g["seq"], D_MODEL) for g in groups)
```

```python
import functools
import math

import jax
import jax.numpy as jnp
import numpy as np
from jax import lax
from jax.experimental import pallas as pl
from jax.experimental.pallas import tpu as pltpu

D_MODEL = 1024
HEAD_DIM = 64
GRID_W = 64
EPS = 1e-6
A_HEADS = 6
A_PATTERNS = ((128, 1), (512, 4), (2048, 16))
ROPE_THETA = 500000.0
ROPE_DIMS = HEAD_DIM // 4
B_HEADS = 4
B_KV_HEADS = 2
AXIAL_THETA = 10000.0
C_HEADS = 6
NA_ROWS = 8
NA_COLS = 16

A_W = A_HEADS * HEAD_DIM
B_W = B_HEADS * HEAD_DIM
B_KV_W = B_KV_HEADS * HEAD_DIM
C_W = C_HEADS * HEAD_DIM
MIX_W = A_W + B_W + C_W
QKV_W = 3 * A_W
OFF_A = 0
OFF_AZ = OFF_A + QKV_W
OFF_B = OFF_AZ + A_W
OFF_BZ = OFF_B + B_W + 2 * B_KV_W
OFF_C = OFF_BZ + B_W
OFF_CZ = OFF_C + QKV_W

LANES = 128
A_TILES = A_W // LANES
Q_SCALE = HEAD_DIM ** -0.5
LOG2E = math.log2(math.e)
VMEM_LIMIT = 56 * 1024 * 1024

IN_ROW_TILE = 512
OUT_ROW_TILE = 1024
A_SUB = 128
A_STEP_TOKENS = 2048
A1_Q_BLOCK = 1024
B_TQ = 512
B_TK = 2048
C_ROWS = 4
C_WIN = 12
C_TQ = C_ROWS * GRID_W
C_STEP_TILES = 8
C_NK = C_WIN * GRID_W
NEG_INF = float("-inf")

_F32 = jnp.float32
_BF16 = jnp.bfloat16


def _cparams(n_grid):
    return pltpu.CompilerParams(dimension_semantics=("arbitrary",) * n_grid,
                                vmem_limit_bytes=VMEM_LIMIT)


def _lane_lo(shape):
    return lax.broadcasted_iota(jnp.int32, shape, len(shape) - 1) % LANES < HEAD_DIM


def _stack_heads(q):
    lo = _lane_lo(q.shape)
    zero = jnp.zeros_like(q)
    return jnp.concatenate([jnp.where(lo, q, zero), jnp.where(lo, zero, q)], axis=0)


def _unstack_heads(o2):
    n = o2.shape[0] // 2
    return jnp.where(_lane_lo((n, o2.shape[1])), o2[:n], o2[n:])


def _dot_nt(a, b):
    return lax.dot_general(a, b, (((1,), (1,)), ((), ())), preferred_element_type=_F32)


def _gated_norm(y, z, gain):
    z = z.astype(_F32)
    u = y * (z * jax.nn.sigmoid(z))
    r = lax.rsqrt(jnp.mean(u * u, axis=-1, keepdims=True) + EPS)
    return u * r * gain


def _rotate_tile(t, tab_ref, shift):
    up = pltpu.roll(t, LANES - shift, axis=1)
    dn = pltpu.roll(t, shift, axis=1)
    return t * tab_ref[0] + up * tab_ref[1] + dn * tab_ref[2]


def _inproj_kernel(x_ref, g_ref, w_ref, ta_ref, tb_ref, gq_ref, gk_ref, bd_ref,
                   a1_ref, a4_ref, a16_ref, c_ref, bq_ref, bkv_ref, az_ref, bz_ref, cz_ref, scr_ref):
    x = x_ref[...]
    tm = x.shape[0]
    r = lax.rsqrt(jnp.mean(x * x, axis=-1, keepdims=True) + EPS)
    h = (x * r * g_ref[...]).astype(_BF16)
    p = jnp.dot(h, w_ref[...], preferred_element_type=_F32)

    for j in range(QKV_W // LANES):
        t = p[:, OFF_A + j * LANES:OFF_A + (j + 1) * LANES]
        if j < 2 * A_TILES:
            t = _rotate_tile(t, ta_ref, ROPE_DIMS // 2)
        if j < A_TILES:
            t = t * (Q_SCALE * LOG2E)
        a1_ref[:, j * LANES:(j + 1) * LANES] = t.astype(_BF16)
        scr_ref[j] = t
    for dil, ref in ((4, a4_ref), (16, a16_ref)):
        for res in range(dil):
            for j in range(QKV_W // LANES):
                rows = scr_ref[j, pl.ds(res, tm // dil, stride=dil), :]
                ref[res, :, j * LANES:(j + 1) * LANES] = rows.astype(_BF16)

    c_ref[:, 0:C_W] = (p[:, OFF_C:OFF_C + C_W] * (Q_SCALE * LOG2E)).astype(_BF16)
    c_ref[:, C_W:QKV_W] = p[:, OFF_C + C_W:OFF_C + QKV_W].astype(_BF16)

    bd = bd_ref[...]

    def headnorm(t, gain):
        width = t.shape[1]
        sq = t * t
        hi = sq.astype(_BF16)
        lo = (sq - hi.astype(_F32)).astype(_BF16)
        ms = (jnp.dot(hi, bd[:width, :width], preferred_element_type=_F32)
              + jnp.dot(lo, bd[:width, :width], preferred_element_type=_F32))
        return t * lax.rsqrt(ms + EPS) * gain

    qn = headnorm(p[:, OFF_B:OFF_B + B_W], jnp.concatenate([gq_ref[...]] * (B_W // LANES), axis=1))
    for j in range(B_W // LANES):
        t = _rotate_tile(qn[:, j * LANES:(j + 1) * LANES], tb_ref, HEAD_DIM // 4) * (Q_SCALE * LOG2E)
        bq_ref[:, j * LANES:(j + 1) * LANES] = t.astype(_BF16)
    lo_half = _lane_lo((tm, LANES))
    kn = _rotate_tile(headnorm(p[:, OFF_B + B_W:OFF_B + B_W + B_KV_W], gk_ref[...]), tb_ref, HEAD_DIM // 4)
    k_sw = pltpu.roll(kn, HEAD_DIM, axis=1)
    bkv_ref[:, 0:LANES] = jnp.where(lo_half, kn, k_sw).astype(_BF16)
    bkv_ref[:, LANES:2 * LANES] = jnp.where(lo_half, k_sw, kn).astype(_BF16)
    vv = p[:, OFF_B + B_W + B_KV_W:OFF_B + B_W + 2 * B_KV_W]
    bkv_ref[:, B_W:B_W + LANES] = jnp.where(lo_half, vv, 1.0).astype(_BF16)
    bkv_ref[:, B_W + LANES:2 * B_W] = jnp.where(lo_half, pltpu.roll(vv, HEAD_DIM, axis=1), 1.0).astype(_BF16)

    az_ref[...] = p[:, OFF_AZ:OFF_AZ + A_W].astype(_BF16)
    bz_ref[...] = p[:, OFF_BZ:OFF_BZ + B_W].astype(_BF16)
    cz_ref[...] = p[:, OFF_CZ:OFF_CZ + C_W].astype(_BF16)


def _inproj(x2d, seq, norm_pre, w, tab_a, tab_b, gq, gk, bd):
    T = x2d.shape[0]
    tm = min(IN_ROW_TILE, seq)
    n_seq = seq // tm
    nw = w.shape[1]
    const = lambda i: (0, 0)
    row = lambda i: (i, 0)
    row3 = lambda i: (0, i, 0)
    tab = lambda i: (0, i % n_seq, 0)
    out_shape = (
        jax.ShapeDtypeStruct((T, QKV_W), _BF16),
        jax.ShapeDtypeStruct((4, T // 4, QKV_W), _BF16),
        jax.ShapeDtypeStruct((16, T // 16, QKV_W), _BF16),
        jax.ShapeDtypeStruct((T, QKV_W), _BF16),
        jax.ShapeDtypeStruct((T, B_W), _BF16),
        jax.ShapeDtypeStruct((T, 2 * B_W), _BF16),
        jax.ShapeDtypeStruct((T, A_W), _BF16),
        jax.ShapeDtypeStruct((T, B_W), _BF16),
        jax.ShapeDtypeStruct((T, C_W), _BF16),
    )
    out_specs = (
        pl.BlockSpec((tm, QKV_W), row),
        pl.BlockSpec((4, tm // 4, QKV_W), row3),
        pl.BlockSpec((16, tm // 16, QKV_W), row3),
        pl.BlockSpec((tm, QKV_W), row),
        pl.BlockSpec((tm, B_W), row),
        pl.BlockSpec((tm, 2 * B_W), row),
        pl.BlockSpec((tm, A_W), row),
        pl.BlockSpec((tm, B_W), row),
        pl.BlockSpec((tm, C_W), row),
    )
    in_specs = [
        pl.BlockSpec((tm, D_MODEL), row),
        pl.BlockSpec((1, D_MODEL), const),
        pl.BlockSpec((D_MODEL, nw), const, pipeline_mode=pl.Buffered(1)),
        pl.BlockSpec((3, tm, LANES), tab),
        pl.BlockSpec((3, tm, LANES), tab),
        pl.BlockSpec((1, LANES), const),
        pl.BlockSpec((1, LANES), const),
        pl.BlockSpec((B_W, B_W), const),
    ]
    return pl.pallas_call(
        _inproj_kernel,
        grid=(T // tm,),
        in_specs=in_specs,
        out_specs=out_specs,
        out_shape=out_shape,
        scratch_shapes=[pltpu.VMEM((QKV_W // LANES, tm, LANES), _F32)],
        compiler_params=_cparams(1),
        name="inproj",
    )(x2d, norm_pre, w, tab_a, tab_b, gq, gk, bd)


def _softmax_pv(q2, kp, vp, bias):
    s = _dot_nt(q2, kp) + bias
    m = jnp.max(s, axis=1, keepdims=True)
    e = jnp.exp2(s - m).astype(_BF16)
    ol = jnp.dot(e, jnp.concatenate([vp, jnp.ones_like(vp)], axis=1), preferred_element_type=_F32)
    l = ol[:, LANES:]
    return ol[:, :LANES] / l, m + jnp.log2(l)


def _band_tile(q, k_ref, v_ref, kidx, kstart, nk, cols, bias):
    kp = k_ref[kidx + (pl.ds(kstart, nk), cols)]
    vp = v_ref[kidx + (pl.ds(kstart, nk), cols)]
    o2, lse2 = _softmax_pv(_stack_heads(q), kp, vp, bias)
    return _unstack_heads(o2), _unstack_heads(lse2)


def _band_delta(sub, nk):
    row = lax.broadcasted_iota(jnp.int32, (2 * sub, nk), 0) % sub
    col = lax.broadcasted_iota(jnp.int32, (2 * sub, nk), 1)
    return col - row


def _band_bias(delta, shift, radius):
    rel = delta + shift
    return jnp.where((rel >= -radius) & (rel <= radius), 0.0, NEG_INF).astype(_F32)


def _attn_a_dil_kernel(q_ref, k_ref, v_ref, o_ref, lse_ref, *, dil, q_block, seq_len, radius, sub):
    nk = min(sub + 2 * radius, seq_len)
    base = pl.program_id(1) * q_block
    delta = _band_delta(sub, nk)
    windows = []
    for si in range(q_block // sub):
        l0 = base + si * sub
        kstart = pl.multiple_of(jnp.clip(l0 - radius, 0, seq_len - nk), HEAD_DIM)
        windows.append((kstart, _band_bias(delta, kstart - l0, radius)))

    def body(res, carry):
        for si, (kstart, bias) in enumerate(windows):
            for p in range(A_TILES):
                cols = slice(p * LANES, (p + 1) * LANES)
                q = q_ref[res, si * sub:(si + 1) * sub, cols]
                o, lse = _band_tile(q, k_ref, v_ref, (res,), kstart, nk, cols, bias)
                rows = pl.ds(si * sub * dil + res, sub, stride=dil)
                o_ref[p, rows, :] = o
                lse_ref[p, rows, :] = lse
        return carry

    def pair_body(rp, carry):
        for p in range(A_TILES):
            cols = slice(p * LANES, (p + 1) * LANES)
            res = (2 * rp, 2 * rp + 1)
            q2 = jnp.concatenate([_stack_heads(q_ref[r, :, cols]) for r in res], axis=0)
            kp = jnp.concatenate([k_ref[r, :, cols] for r in res], axis=0)
            vp = jnp.concatenate([v_ref[r, :, cols] for r in res], axis=0)
            o2, lse2 = _softmax_pv(q2, kp, vp, pair_bias)
            for i, r in enumerate(res):
                rows = pl.ds(r, sub, stride=dil)
                o_ref[p, rows, :] = _unstack_heads(o2[i * 2 * sub:(i + 1) * 2 * sub])
                lse_ref[p, rows, :] = _unstack_heads(lse2[i * 2 * sub:(i + 1) * 2 * sub])
        return carry

    if seq_len == q_block == sub and 2 * nk <= 2 * LANES and dil % 2 == 0:
        bias = windows[0][1]
        neg = jnp.full_like(bias, NEG_INF)
        pair_bias = jnp.concatenate([jnp.concatenate([bias, neg], axis=1),
                                     jnp.concatenate([neg, bias], axis=1)], axis=0)
        lax.fori_loop(0, dil // 2, pair_body, 0, unroll=min(dil // 2, 4))
    else:
        lax.fori_loop(0, dil, body, 0, unroll=max(1, min(dil, 8 * sub // q_block)))


def _attn_a_dil(a_res, batch, seq, dil, radius):
    L = seq // dil
    tl = min(max(A_STEP_TOKENS // dil, A_SUB), L)
    sub = min(A_SUB, tl)
    n_lt = L // tl
    T = batch * seq
    kern = functools.partial(_attn_a_dil_kernel, dil=dil, q_block=tl, seq_len=L, radius=radius, sub=sub)
    q_spec = pl.BlockSpec((dil, tl, A_W), lambda b, t: (0, b * n_lt + t, 0))
    k_spec = pl.BlockSpec((dil, L, A_W), lambda b, t: (0, b, 1))
    v_spec = pl.BlockSpec((dil, L, A_W), lambda b, t: (0, b, 2))
    o_spec = pl.BlockSpec((A_TILES, tl * dil, LANES), lambda b, t: (0, b * n_lt + t, 0))
    out_shape = (jax.ShapeDtypeStruct((A_TILES, T, LANES), _F32),) * 2
    return pl.pallas_call(
        kern,
        grid=(batch, n_lt),
        in_specs=[q_spec, k_spec, v_spec],
        out_specs=(o_spec, o_spec),
        out_shape=out_shape,
        compiler_params=_cparams(2),
        name=f"attn_a_d{dil}",
    )(a_res, a_res, a_res)


def _attn_a_final_kernel(q_ref, k_ref, v_ref, o4_ref, l4_ref, o16_ref, l16_ref, z_ref, g_ref, y_ref,
                         *, q_block, seq_len, radius, sub):
    nk = min(sub + 2 * radius, seq_len)
    n_sub = q_block // sub
    base = pl.program_id(1) * q_block
    delta = _band_delta(sub, nk)
    inner_bias = _band_bias(delta, -radius, radius)
    for si in range(n_sub):
        l0 = base + si * sub
        kstart = pl.multiple_of(jnp.clip(l0 - radius, 0, seq_len - nk), HEAD_DIM)
        inner = nk == sub + 2 * radius and 0 < si < n_sub - 1
        bias = inner_bias if inner else _band_bias(delta, kstart - l0, radius)
        rows = slice(si * sub, (si + 1) * sub)
        tiles = []
        for p in range(A_TILES):
            cols = slice(p * LANES, (p + 1) * LANES)
            o1, l1 = _band_tile(q_ref[rows, cols], k_ref, v_ref, (), kstart, nk, cols, bias)
            l4, l16 = l4_ref[p, rows, :], l16_ref[p, rows, :]
            m = jnp.maximum(jnp.maximum(l1, l4), l16)
            e1, e4, e16 = jnp.exp2(l1 - m), jnp.exp2(l4 - m), jnp.exp2(l16 - m)
            mix = e1 * o1 + e4 * o4_ref[p, rows, :] + e16 * o16_ref[p, rows, :]
            tiles.append(mix / (e1 + e4 + e16))
        ya = jnp.concatenate(tiles, axis=1)
        y_ref[rows, :] = _gated_norm(ya, z_ref[rows, :], g_ref[...]).astype(_BF16)


def _attn_a_final(a1, o4, l4, o16, l16, az, gain, batch, seq, radius):
    T = a1.shape[0]
    tl = min(A1_Q_BLOCK, seq)
    sub = min(A_SUB, tl)
    n_lt = seq // tl
    kern = functools.partial(_attn_a_final_kernel, q_block=tl, seq_len=seq, radius=radius, sub=sub)
    tok = lambda b, t: (b * n_lt + t, 0)
    tok3 = lambda b, t: (0, b * n_lt + t, 0)
    st_spec = pl.BlockSpec((A_TILES, tl, LANES), tok3)
    return pl.pallas_call(
        kern,
        grid=(batch, n_lt),
        in_specs=[
            pl.BlockSpec((tl, A_W), tok),
            pl.BlockSpec((seq, A_W), lambda b, t: (b, 1)),
            pl.BlockSpec((seq, A_W), lambda b, t: (b, 2)),
            st_spec, st_spec, st_spec, st_spec,
            pl.BlockSpec((tl, A_W), tok),
            pl.BlockSpec((1, A_W), lambda b, t: (0, 0)),
        ],
        out_specs=pl.BlockSpec((tl, A_W), tok),
        out_shape=jax.ShapeDtypeStruct((T, A_W), _BF16),
        compiler_params=_cparams(2),
        name="attn_a_d1",
    )(a1, a1, a1, o4, l4, o16, l16, az, gain)


def _attn_b_kernel(q_ref, kv_ref, z_ref, g_ref, y_ref, *, seq, tk):
    n_grp = B_W // LANES
    q2s = [_stack_heads(q_ref[:, g * LANES:(g + 1) * LANES]) for g in range(n_grp)]
    n2 = q2s[0].shape[0]

    def body(kt, carry):
        ks = pl.multiple_of(kt * tk, tk)
        out = []
        for g in range(n_grp):
            m, acc = carry[g]
            k = kv_ref[pl.ds(ks, tk), g * LANES:(g + 1) * LANES]
            v = kv_ref[pl.ds(ks, tk), B_W + g * LANES:B_W + (g + 1) * LANES]
            s = _dot_nt(q2s[g], k)
            m_new = jnp.maximum(m, jnp.max(s, axis=1, keepdims=True))
            alpha = jnp.exp2(m - m_new)
            p = jnp.exp2(s - m_new)
            acc = alpha * acc + jnp.dot(p.astype(_BF16), v, preferred_element_type=_F32)
            out.append((m_new, acc))
        return tuple(out)

    init = tuple((jnp.full((n2, 1), NEG_INF, _F32), jnp.zeros((n2, LANES), _F32)) for _ in range(n_grp))
    final = lax.fori_loop(0, seq // tk, body, init)
    tq = n2 // 2
    tiles = []
    for g in range(n_grp):
        acc = final[g][1]
        head0, head1 = acc[:tq], acc[tq:]
        tiles.append(jnp.where(_lane_lo((tq, LANES)), head0 / pltpu.roll(head0, HEAD_DIM, axis=1),
                               pltpu.roll(head1, HEAD_DIM, axis=1) / head1))
    yb = jnp.concatenate(tiles, axis=1)
    y_ref[...] = _gated_norm(yb, z_ref[...], g_ref[...]).astype(_BF16)


def _attn_b(bq, bkv, bz, gain, batch, seq):
    T = bq.shape[0]
    tq = min(B_TQ, seq)
    tk = min(B_TK, seq)
    nq = seq // tq
    kern = functools.partial(_attn_b_kernel, seq=seq, tk=tk)
    tok = lambda b, t: (b * nq + t, 0)
    return pl.pallas_call(
        kern,
        grid=(batch, nq),
        in_specs=[
            pl.BlockSpec((tq, B_W), tok),
            pl.BlockSpec((seq, 2 * B_W), lambda b, t: (b, 0)),
            pl.BlockSpec((tq, B_W), tok),
            pl.BlockSpec((1, B_W), lambda b, t: (0, 0)),
        ],
        out_specs=pl.BlockSpec((tq, B_W), tok),
        out_shape=jax.ShapeDtypeStruct((T, B_W), _BF16),
        compiler_params=_cparams(2),
        name="attn_b",
    )(bq, bkv, bz, gain)


def _c_tile_types():
    half = NA_ROWS // 2
    first = dict(ro_base=NA_ROWS - 1, start=[0] * C_ROWS)
    inner = dict(ro_base=NA_ROWS - 1 - half, start=list(range(C_ROWS)))
    last = dict(ro_base=NA_ROWS - 1 - (C_WIN - C_ROWS), start=[C_WIN - NA_ROWS] * C_ROWS)
    return (first, inner, last)


def _bias_kernel(rb_ref, out_ref):
    h = pl.program_id(0)
    n_ro = 2 * NA_ROWS - 1
    n_co = 2 * NA_COLS - 1
    qi = lax.broadcasted_iota(jnp.int32, (GRID_W, LANES), 0)
    lane = lax.broadcasted_iota(jnp.int32, (GRID_W, LANES), 1)
    kc = lane % GRID_W
    d = kc - qi
    cs = jnp.clip(qi - NA_COLS // 2, 0, GRID_W - NA_COLS)
    col_valid = (kc >= cs) & (kc < cs + NA_COLS)
    hits = [d == co - (NA_COLS - 1) for co in range(n_co)]
    neg = jnp.full((GRID_W, LANES), NEG_INF, _F32)
    rows = []
    for ro in range(n_ro):
        e = neg
        for co in range(n_co):
            e = jnp.where(hits[co], rb_ref[(h * n_ro + ro) * n_co + co] * LOG2E, e)
        rows.append(jnp.where(col_valid, e, neg))
    lo = lane < GRID_W
    for t, spec in enumerate(_c_tile_types()):
        for j in range(C_ROWS):
            def blk(kk):
                ok = spec["start"][j] <= kk < spec["start"][j] + NA_ROWS
                return rows[kk - j + spec["ro_base"]] if ok else neg
            tiles = [jnp.where(lo, blk(2 * i), blk(2 * i + 1)) for i in range(C_WIN // 2)]
            out_ref[t, j * GRID_W:(j + 1) * GRID_W, :] = jnp.concatenate(tiles, axis=1)


def _bias_table(rel_bias):
    flat = rel_bias.reshape(-1)
    return pl.pallas_call(
        _bias_kernel,
        grid=(C_HEADS,),
        in_specs=[pl.BlockSpec(memory_space=pltpu.SMEM)],
        out_specs=pl.BlockSpec((3, None, C_TQ, C_NK), lambda h: (0, h, 0, 0)),
        out_shape=jax.ShapeDtypeStruct((3, C_HEADS, C_TQ, C_NK), _F32),
        compiler_params=_cparams(1),
        name="c_bias",
    )(flat)


def _attn_c_kernel(q_ref, k_ref, v_ref, b_first_ref, b_mid_ref, b_last_ref, z_ref, g_ref, y_ref, *, grid_rows):
    b_refs = [b_first_ref] + [b_mid_ref] * (C_STEP_TILES - 2) + [b_last_ref]
    for h in range(C_STEP_TILES):
        r0 = (pl.program_id(1) * C_STEP_TILES + h) * C_ROWS
        ws = jnp.clip(r0 - NA_ROWS // 2, 0, grid_rows - C_WIN)
        ks = pl.multiple_of(ws * GRID_W, GRID_W)
        rows = slice(h * C_TQ, (h + 1) * C_TQ)
        tiles = []
        for p in range(C_W // LANES):
            cols = slice(p * LANES, (p + 1) * LANES)
            q2 = _stack_heads(q_ref[rows, cols])
            s = _dot_nt(q2, k_ref[pl.ds(ks, C_NK), cols]) + b_refs[h][p]
            m = jnp.max(s, axis=1, keepdims=True)
            e = jnp.exp2(s - m).astype(_BF16)
            vp = v_ref[pl.ds(ks, C_NK), cols]
            ol = jnp.dot(e, jnp.concatenate([vp, jnp.ones_like(vp)], axis=1), preferred_element_type=_F32)
            tiles.append(_unstack_heads(ol[:, :LANES] / ol[:, LANES:]))
        yc = jnp.concatenate(tiles, axis=1)
        y_ref[rows, :] = _gated_norm(yc, z_ref[rows, :], g_ref[...]).astype(_BF16)


def _attn_c(cqkv, bias, cz, gain, batch, seq):
    T = cqkv.shape[0]
    grid_rows = seq // GRID_W
    n_tiles = seq // C_TQ
    assert grid_rows >= C_WIN and n_tiles % C_STEP_TILES == 0
    nq = n_tiles // C_STEP_TILES
    tq = C_STEP_TILES * C_TQ
    n_pair = C_W // LANES
    bias = bias.reshape(3, n_pair, 2 * C_TQ, C_NK)

    def bias_spec(index):
        return pl.BlockSpec((None, n_pair, 2 * C_TQ, C_NK), index, pipeline_mode=pl.Buffered(1))

    bias_specs = [bias_spec(lambda b, t: (jnp.where(t == 0, 0, 1), 0, 0, 0)),
                  bias_spec(lambda b, t: (1, 0, 0, 0)),
                  bias_spec(lambda b, t: (jnp.where(t == nq - 1, 2, 1), 0, 0, 0))]
    kern = functools.partial(_attn_c_kernel, grid_rows=grid_rows)
    tok = lambda b, t: (b * nq + t, 0)
    resident = lambda col: pl.BlockSpec((seq, C_W), lambda b, t: (b, col), pipeline_mode=pl.Buffered(1))
    return pl.pallas_call(
        kern,
        grid=(batch, nq),
        in_specs=[pl.BlockSpec((tq, C_W), tok), resident(1), resident(2)] + bias_specs
        + [pl.BlockSpec((tq, C_W), tok), pl.BlockSpec((1, C_W), lambda b, t: (0, 0))],
        out_specs=pl.BlockSpec((tq, C_W), tok),
        out_shape=jax.ShapeDtypeStruct((T, C_W), _BF16),
        compiler_params=_cparams(2),
        name="attn_c",
    )(cqkv, cqkv, cqkv, bias, bias, bias, cz, gain)


def _outproj_kernel(ya_ref, yb_ref, yc_ref, w_ref, npost_ref, x_ref, out_ref):
    y = jnp.concatenate([ya_ref[...], yb_ref[...], yc_ref[...]], axis=1)
    t = jnp.dot(y, w_ref[...], preferred_element_type=_F32)
    r = lax.rsqrt(jnp.mean(t * t, axis=-1, keepdims=True) + EPS)
    out_ref[...] = x_ref[...] + t * r * npost_ref[...]


def _outproj(ya, yb, yc, w, npost, x2d):
    T = x2d.shape[0]
    tm = min(OUT_ROW_TILE, T)
    row = lambda i: (i, 0)
    const = lambda i: (0, 0)
    return pl.pallas_call(
        _outproj_kernel,
        grid=(T // tm,),
        in_specs=[pl.BlockSpec((tm, A_W), row), pl.BlockSpec((tm, B_W), row), pl.BlockSpec((tm, C_W), row),
                  pl.BlockSpec((MIX_W, D_MODEL), const, pipeline_mode=pl.Buffered(1)),
                  pl.BlockSpec((1, D_MODEL), const), pl.BlockSpec((tm, D_MODEL), row)],
        out_specs=pl.BlockSpec((tm, D_MODEL), row),
        out_shape=jax.ShapeDtypeStruct((T, D_MODEL), _F32),
        compiler_params=_cparams(1),
        name="outproj",
    )(ya, yb, yc, w, npost, x2d)


def _lane_freqs(freqs, n_rot, width):
    lane = np.arange(LANES) % width
    idx = lane % (n_rot // 2)
    rot = lane < n_rot
    first = rot & (lane % n_rot < n_rot // 2)
    second = rot & ~first
    return jnp.where(rot, freqs[idx], 0.0), first, second


def _rotary_tables(ang, first, second):
    cos, sin = jnp.cos(ang), jnp.sin(ang)
    zero = jnp.zeros_like(sin)
    return jnp.stack([cos, jnp.where(first, -sin, zero), jnp.where(second, sin, zero)])


def _rope_tables(seq):
    freqs = ROPE_THETA ** (-jnp.arange(0, ROPE_DIMS, 2, dtype=_F32) / ROPE_DIMS)
    lane_f, first, second = _lane_freqs(freqs, ROPE_DIMS, HEAD_DIM)
    ang = jnp.arange(seq, dtype=_F32)[:, None] * lane_f[None, :]
    return _rotary_tables(ang, first, second)


def _axial_tables(seq):
    half = HEAD_DIM // 2
    freqs = AXIAL_THETA ** (-jnp.arange(0, half, 2, dtype=_F32) / half)
    lane_f, first, second = _lane_freqs(freqs, half, half)
    t = jnp.arange(seq)
    by_row = (np.arange(LANES) % HEAD_DIM) < half
    pos = jnp.where(by_row[None, :], (t // GRID_W).astype(_F32)[:, None], (t % GRID_W).astype(_F32)[:, None])
    return _rotary_tables(pos * lane_f[None, :], first, second)


def _layer(x2d, batch, seq, tabs, norm_pre, w_in, gq, gk, bd, bias, gains, w_out, npost):
    tab_a, tab_b = tabs
    ga, gb, gc = gains
    a1, a4, a16, cqkv, bq, bkv, az, bz, cz = _inproj(x2d, seq, norm_pre, w_in, tab_a, tab_b, gq, gk, bd)
    radii = {dil: window // (2 * dil) for window, dil in A_PATTERNS}
    o4, l4 = _attn_a_dil(a4, batch, seq, 4, radii[4])
    o16, l16 = _attn_a_dil(a16, batch, seq, 16, radii[16])
    ya = _attn_a_final(a1, o4, l4, o16, l16, az, ga, batch, seq, radii[1])
    yb = _attn_b(bq, bkv, bz, gb, batch, seq)
    yc = _attn_c(cqkv, bias, cz, gc, batch, seq)
    return _outproj(ya, yb, yc, w_out, npost, x2d)


def kernel(x_prompt, x_sample, norm_pre, w_in, q_norm, k_norm, rel_bias, branch_gain, w_out, norm_post):
    depth = w_in.shape[0]
    w_in_p = w_in.astype(_BF16)
    w_out_p = w_out.astype(_BF16)
    bd = jnp.asarray(np.kron(np.eye(B_W // HEAD_DIM), np.full((HEAD_DIM, HEAD_DIM), 1.0 / HEAD_DIM)), _BF16)
    groups = []
    for x in (x_prompt, x_sample):
        batch, seq, _ = x.shape
        groups.append(dict(y=x.reshape(batch * seq, D_MODEL), batch=batch, seq=seq,
                           tabs=(_rope_tables(seq), _axial_tables(seq))))
    for l in range(depth):
        bias = _bias_table(rel_bias[l])
        gq = jnp.tile(q_norm[l][None, :], (1, LANES // HEAD_DIM))
        gk = jnp.tile(k_norm[l][None, :], (1, LANES // HEAD_DIM))
        gain = branch_gain[l][None, :]
        gains = (gain[:, 0:A_W], gain[:, A_W:A_W + B_W], gain[:, A_W + B_W:MIX_W])
        for g in groups:
            g["y"] = _layer(g["y"], g["batch"], g["seq"], g["tabs"], norm_pre[l][None, :], w_in_p[l],
                            gq, gk, bd, bias, gains, w_out_p[l], norm_post[l][None, :])
    return tuple(g["y"].reshape(g["batch"], g["seq"], D_MODEL) for g in groups)
```

```python
import functools
import math

import jax
import jax.numpy as jnp
import numpy as np
from jax import lax
from jax.experimental import pallas as pl
from jax.experimental.pallas import tpu as pltpu

D_MODEL = 1024
HEAD_DIM = 64
GRID_W = 64
EPS = 1e-6
A_HEADS = 6
A_PATTERNS = ((128, 1), (512, 4), (2048, 16))
ROPE_THETA = 500000.0
ROPE_DIMS = HEAD_DIM // 4
B_HEADS = 4
B_KV_HEADS = 2
AXIAL_THETA = 10000.0
C_HEADS = 6
NA_ROWS = 8
NA_COLS = 16

A_W = A_HEADS * HEAD_DIM
B_W = B_HEADS * HEAD_DIM
B_KV_W = B_KV_HEADS * HEAD_DIM
C_W = C_HEADS * HEAD_DIM
MIX_W = A_W + B_W + C_W
QKV_W = 3 * A_W
OFF_A = 0
OFF_AZ = OFF_A + QKV_W
OFF_B = OFF_AZ + A_W
OFF_BZ = OFF_B + B_W + 2 * B_KV_W
OFF_C = OFF_BZ + B_W
OFF_CZ = OFF_C + QKV_W

LANES = 128
A_TILES = A_W // LANES
Q_SCALE = HEAD_DIM ** -0.5
LOG2E = math.log2(math.e)
VMEM_LIMIT = 56 * 1024 * 1024

IN_ROW_TILE = 512
OUT_ROW_TILE = 1024
A_SUB = 128
A_STEP_TOKENS = 2048
A1_Q_BLOCK = 1024
B_TQ = 512
B_TK = 2048
C_ROWS = 4
C_WIN = 12
C_TQ = C_ROWS * GRID_W
C_STEP_TILES = 8
C_NK = C_WIN * GRID_W
NEG_INF = float("-inf")

_F32 = jnp.float32
_BF16 = jnp.bfloat16


def _cparams(n_grid):
    return pltpu.CompilerParams(dimension_semantics=("arbitrary",) * n_grid,
                                vmem_limit_bytes=VMEM_LIMIT)


def _lane_lo(shape):
    return lax.broadcasted_iota(jnp.int32, shape, len(shape) - 1) % LANES < HEAD_DIM


def _stack_heads(q):
    lo = _lane_lo(q.shape)
    zero = jnp.zeros_like(q)
    return jnp.concatenate([jnp.where(lo, q, zero), jnp.where(lo, zero, q)], axis=0)


def _dot_nt(a, b):
    return lax.dot_general(a, b, (((1,), (1,)), ((), ())), preferred_element_type=_F32)


def _gated_norm(y, z, gain):
    hz = 0.5 * z.astype(_F32)
    u = y * (hz + hz * jnp.tanh(hz))
    r = lax.rsqrt(jnp.mean(u * u, axis=-1, keepdims=True) + EPS)
    return u * r * gain


def _rotate_tile(t, tab_ref, shift):
    up = pltpu.roll(t, LANES - shift, axis=1)
    dn = pltpu.roll(t, shift, axis=1)
    return t * tab_ref[0] + up * tab_ref[1] + dn * tab_ref[2]


def _inproj_kernel(x_ref, g_ref, w_ref, ta_ref, tb_ref, gq_ref, gk_ref, bd_ref,
                   a1_ref, a4_ref, a16_ref, c_ref, bq_ref, bkv_ref, az_ref, bz_ref, cz_ref, scr_ref):
    x = x_ref[...]
    tm = x.shape[0]
    r = lax.rsqrt(jnp.mean(x * x, axis=-1, keepdims=True) + EPS)
    h = (x * r * g_ref[...]).astype(_BF16)
    p = jnp.dot(h, w_ref[...], preferred_element_type=_F32)

    for j in range(QKV_W // LANES):
        t = p[:, OFF_A + j * LANES:OFF_A + (j + 1) * LANES]
        if j < 2 * A_TILES:
            t = _rotate_tile(t, ta_ref, ROPE_DIMS // 2)
        if j < A_TILES:
            t = t * (Q_SCALE * LOG2E)
        a1_ref[:, j * LANES:(j + 1) * LANES] = t.astype(_BF16)
        scr_ref[j] = t
    for dil, ref in ((4, a4_ref), (16, a16_ref)):
        for res in range(dil):
            for j in range(QKV_W // LANES):
                rows = scr_ref[j, pl.ds(res, tm // dil, stride=dil), :]
                ref[res, :, j * LANES:(j + 1) * LANES] = rows.astype(_BF16)

    c_ref[:, 0:C_W] = (p[:, OFF_C:OFF_C + C_W] * (Q_SCALE * LOG2E)).astype(_BF16)
    c_ref[:, C_W:QKV_W] = p[:, OFF_C + C_W:OFF_C + QKV_W].astype(_BF16)

    bd = bd_ref[...]

    def headnorm(t, gain):
        width = t.shape[1]
        sq = t * t
        hi = sq.astype(_BF16)
        lo = (sq - hi.astype(_F32)).astype(_BF16)
        ms = (jnp.dot(hi, bd[:width, :width], preferred_element_type=_F32)
              + jnp.dot(lo, bd[:width, :width], preferred_element_type=_F32))
        return t * lax.rsqrt(ms + EPS) * gain

    qn = headnorm(p[:, OFF_B:OFF_B + B_W], jnp.concatenate([gq_ref[...]] * (B_W // LANES), axis=1))
    for j in range(B_W // LANES):
        t = _rotate_tile(qn[:, j * LANES:(j + 1) * LANES], tb_ref, HEAD_DIM // 4) * (Q_SCALE * LOG2E)
        bq_ref[:, j * LANES:(j + 1) * LANES] = t.astype(_BF16)
    lo_half = _lane_lo((tm, LANES))
    kn = _rotate_tile(headnorm(p[:, OFF_B + B_W:OFF_B + B_W + B_KV_W], gk_ref[...]), tb_ref, HEAD_DIM // 4)
    k_sw = pltpu.roll(kn, HEAD_DIM, axis=1)
    bkv_ref[:, 0:LANES] = jnp.where(lo_half, kn, k_sw).astype(_BF16)
    bkv_ref[:, LANES:2 * LANES] = jnp.where(lo_half, k_sw, kn).astype(_BF16)
    vv = p[:, OFF_B + B_W + B_KV_W:OFF_B + B_W + 2 * B_KV_W]
    bkv_ref[:, B_W:B_W + LANES] = jnp.where(lo_half, vv, 1.0).astype(_BF16)
    bkv_ref[:, B_W + LANES:2 * B_W] = jnp.where(lo_half, pltpu.roll(vv, HEAD_DIM, axis=1), 1.0).astype(_BF16)

    az_ref[...] = p[:, OFF_AZ:OFF_AZ + A_W].astype(_BF16)
    bz_ref[...] = p[:, OFF_BZ:OFF_BZ + B_W].astype(_BF16)
    cz_ref[...] = p[:, OFF_CZ:OFF_CZ + C_W].astype(_BF16)


def _inproj(x2d, seq, norm_pre, w, tab_a, tab_b, gq, gk, bd):
    T = x2d.shape[0]
    tm = min(IN_ROW_TILE, seq)
    n_seq = seq // tm
    nw = w.shape[1]
    const = lambda i: (0, 0)
    row = lambda i: (i, 0)
    row3 = lambda i: (0, i, 0)
    tab = lambda i: (0, i % n_seq, 0)
    out_shape = (
        jax.ShapeDtypeStruct((T, QKV_W), _BF16),
        jax.ShapeDtypeStruct((4, T // 4, QKV_W), _BF16),
        jax.ShapeDtypeStruct((16, T // 16, QKV_W), _BF16),
        jax.ShapeDtypeStruct((T, QKV_W), _BF16),
        jax.ShapeDtypeStruct((T, B_W), _BF16),
        jax.ShapeDtypeStruct((T, 2 * B_W), _BF16),
        jax.ShapeDtypeStruct((T, A_W), _BF16),
        jax.ShapeDtypeStruct((T, B_W), _BF16),
        jax.ShapeDtypeStruct((T, C_W), _BF16),
    )
    out_specs = (
        pl.BlockSpec((tm, QKV_W), row),
        pl.BlockSpec((4, tm // 4, QKV_W), row3),
        pl.BlockSpec((16, tm // 16, QKV_W), row3),
        pl.BlockSpec((tm, QKV_W), row),
        pl.BlockSpec((tm, B_W), row),
        pl.BlockSpec((tm, 2 * B_W), row),
        pl.BlockSpec((tm, A_W), row),
        pl.BlockSpec((tm, B_W), row),
        pl.BlockSpec((tm, C_W), row),
    )
    in_specs = [
        pl.BlockSpec((tm, D_MODEL), row),
        pl.BlockSpec((1, D_MODEL), const),
        pl.BlockSpec((D_MODEL, nw), const, pipeline_mode=pl.Buffered(1)),
        pl.BlockSpec((3, tm, LANES), tab),
        pl.BlockSpec((3, tm, LANES), tab),
        pl.BlockSpec((1, LANES), const),
        pl.BlockSpec((1, LANES), const),
        pl.BlockSpec((B_W, B_W), const),
    ]
    return pl.pallas_call(
        _inproj_kernel,
        grid=(T // tm,),
        in_specs=in_specs,
        out_specs=out_specs,
        out_shape=out_shape,
        scratch_shapes=[pltpu.VMEM((QKV_W // LANES, tm, LANES), _F32)],
        compiler_params=_cparams(1),
        name="inproj",
    )(x2d, norm_pre, w, tab_a, tab_b, gq, gk, bd)


def _softmax_pv(q2, kp, vp, bias):
    s = _dot_nt(q2, kp) + bias
    m = jnp.max(s, axis=1, keepdims=True)
    e = jnp.exp2(s - m).astype(_BF16)
    return jnp.dot(e, jnp.concatenate([vp, jnp.ones_like(vp)], axis=1), preferred_element_type=_F32), m


def _finish_heads(ol, m=None):
    n = ol.shape[0] // 2
    lo = _lane_lo((n, LANES))
    num = jnp.where(lo, ol[:n, :LANES], ol[n:, :LANES])
    den = jnp.where(lo, ol[:n, LANES:], ol[n:, LANES:])
    out = num / den
    if m is None:
        return out
    return out, jnp.where(lo, m[:n], m[n:]) + jnp.log2(den)


def _band_tile(q, k_ref, v_ref, kidx, kstart, nk, cols, bias):
    kp = k_ref[kidx + (pl.ds(kstart, nk), cols)]
    vp = v_ref[kidx + (pl.ds(kstart, nk), cols)]
    return _finish_heads(*_softmax_pv(_stack_heads(q), kp, vp, bias))


def _band_delta(sub, nk):
    row = lax.broadcasted_iota(jnp.int32, (2 * sub, nk), 0) % sub
    col = lax.broadcasted_iota(jnp.int32, (2 * sub, nk), 1)
    return col - row


def _band_bias(delta, shift, radius):
    rel = delta + shift
    return jnp.where((rel >= -radius) & (rel <= radius), 0.0, NEG_INF).astype(_F32)


def _attn_a_dil_kernel(q_ref, k_ref, v_ref, o_ref, lse_ref, *, dil, q_block, seq_len, radius, sub):
    nk = min(sub + 2 * radius, seq_len)
    base = pl.program_id(1) * q_block
    delta = _band_delta(sub, nk)
    windows = []
    for si in range(q_block // sub):
        l0 = base + si * sub
        kstart = pl.multiple_of(jnp.clip(l0 - radius, 0, seq_len - nk), HEAD_DIM)
        windows.append((kstart, _band_bias(delta, kstart - l0, radius)))

    def body(res, carry):
        for si, (kstart, bias) in enumerate(windows):
            for p in range(A_TILES):
                cols = slice(p * LANES, (p + 1) * LANES)
                q = q_ref[res, si * sub:(si + 1) * sub, cols]
                o, lse = _band_tile(q, k_ref, v_ref, (res,), kstart, nk, cols, bias)
                rows = pl.ds(si * sub * dil + res, sub, stride=dil)
                o_ref[p, rows, :] = o
                lse_ref[p, rows, :] = lse
        return carry

    def pair_body(rp, carry):
        for p in range(A_TILES):
            cols = slice(p * LANES, (p + 1) * LANES)
            res = (2 * rp, 2 * rp + 1)
            q2 = jnp.concatenate([_stack_heads(q_ref[r, :, cols]) for r in res], axis=0)
            kp = jnp.concatenate([k_ref[r, :, cols] for r in res], axis=0)
            vp = jnp.concatenate([v_ref[r, :, cols] for r in res], axis=0)
            ol, m = _softmax_pv(q2, kp, vp, pair_bias)
            for i, r in enumerate(res):
                blk = slice(i * 2 * sub, (i + 1) * 2 * sub)
                rows = pl.ds(r, sub, stride=dil)
                o_ref[p, rows, :], lse_ref[p, rows, :] = _finish_heads(ol[blk], m[blk])
        return carry

    if seq_len == q_block == sub and 2 * nk <= 2 * LANES and dil % 2 == 0:
        bias = windows[0][1]
        neg = jnp.full_like(bias, NEG_INF)
        pair_bias = jnp.concatenate([jnp.concatenate([bias, neg], axis=1),
                                     jnp.concatenate([neg, bias], axis=1)], axis=0)
        lax.fori_loop(0, dil // 2, pair_body, 0, unroll=min(dil // 2, 4))
    else:
        lax.fori_loop(0, dil, body, 0, unroll=max(1, min(dil, 8 * sub // q_block)))


def _attn_a_dil(a_res, batch, seq, dil, radius):
    L = seq // dil
    tl = min(max(A_STEP_TOKENS // dil, A_SUB), L)
    sub = min(A_SUB, tl)
    n_lt = L // tl
    T = batch * seq
    kern = functools.partial(_attn_a_dil_kernel, dil=dil, q_block=tl, seq_len=L, radius=radius, sub=sub)
    q_spec = pl.BlockSpec((dil, tl, A_W), lambda b, t: (0, b * n_lt + t, 0))
    k_spec = pl.BlockSpec((dil, L, A_W), lambda b, t: (0, b, 1))
    v_spec = pl.BlockSpec((dil, L, A_W), lambda b, t: (0, b, 2))
    o_spec = pl.BlockSpec((A_TILES, tl * dil, LANES), lambda b, t: (0, b * n_lt + t, 0))
    out_shape = (jax.ShapeDtypeStruct((A_TILES, T, LANES), _F32),) * 2
    return pl.pallas_call(
        kern,
        grid=(batch, n_lt),
        in_specs=[q_spec, k_spec, v_spec],
        out_specs=(o_spec, o_spec),
        out_shape=out_shape,
        compiler_params=_cparams(2),
        name=f"attn_a_d{dil}",
    )(a_res, a_res, a_res)


def _attn_a_final_kernel(q_ref, k_ref, v_ref, o4_ref, l4_ref, o16_ref, l16_ref, z_ref, g_ref, y_ref,
                         *, q_block, seq_len, radius, sub):
    nk = min(sub + 2 * radius, seq_len)
    n_sub = q_block // sub
    base = pl.program_id(1) * q_block
    delta = _band_delta(sub, nk)
    inner_bias = _band_bias(delta, -radius, radius)
    for si in range(n_sub):
        l0 = base + si * sub
        kstart = pl.multiple_of(jnp.clip(l0 - radius, 0, seq_len - nk), HEAD_DIM)
        inner = nk == sub + 2 * radius and 0 < si < n_sub - 1
        bias = inner_bias if inner else _band_bias(delta, kstart - l0, radius)
        rows = slice(si * sub, (si + 1) * sub)
        tiles = []
        for p in range(A_TILES):
            cols = slice(p * LANES, (p + 1) * LANES)
            o1, l1 = _band_tile(q_ref[rows, cols], k_ref, v_ref, (), kstart, nk, cols, bias)
            l4, l16 = l4_ref[p, rows, :], l16_ref[p, rows, :]
            m = jnp.maximum(jnp.maximum(l1, l4), l16)
            e1, e4, e16 = jnp.exp2(l1 - m), jnp.exp2(l4 - m), jnp.exp2(l16 - m)
            mix = e1 * o1 + e4 * o4_ref[p, rows, :] + e16 * o16_ref[p, rows, :]
            tiles.append(mix / (e1 + e4 + e16))
        ya = jnp.concatenate(tiles, axis=1)
        y_ref[rows, :] = _gated_norm(ya, z_ref[rows, :], g_ref[...]).astype(_BF16)


def _attn_a_final(a1, o4, l4, o16, l16, az, gain, batch, seq, radius):
    T = a1.shape[0]
    tl = min(A1_Q_BLOCK, seq)
    sub = min(A_SUB, tl)
    n_lt = seq // tl
    kern = functools.partial(_attn_a_final_kernel, q_block=tl, seq_len=seq, radius=radius, sub=sub)
    tok = lambda b, t: (b * n_lt + t, 0)
    tok3 = lambda b, t: (0, b * n_lt + t, 0)
    st_spec = pl.BlockSpec((A_TILES, tl, LANES), tok3)
    return pl.pallas_call(
        kern,
        grid=(batch, n_lt),
        in_specs=[
            pl.BlockSpec((tl, A_W), tok),
            pl.BlockSpec((seq, A_W), lambda b, t: (b, 1)),
            pl.BlockSpec((seq, A_W), lambda b, t: (b, 2)),
            st_spec, st_spec, st_spec, st_spec,
            pl.BlockSpec((tl, A_W), tok),
            pl.BlockSpec((1, A_W), lambda b, t: (0, 0)),
        ],
        out_specs=pl.BlockSpec((tl, A_W), tok),
        out_shape=jax.ShapeDtypeStruct((T, A_W), _BF16),
        compiler_params=_cparams(2),
        name="attn_a_d1",
    )(a1, a1, a1, o4, l4, o16, l16, az, gain)


def _attn_b_kernel(q_ref, kv_ref, z_ref, g_ref, y_ref, *, seq, tk, tq):
    n_grp = B_W // LANES
    n_kv = seq // tk
    for h in range(q_ref.shape[0] // tq):
        rows = slice(h * tq, (h + 1) * tq)
        q2s = [_stack_heads(q_ref[rows, g * LANES:(g + 1) * LANES]) for g in range(n_grp)]

        def body(kt, carry):
            ks = pl.multiple_of(kt * tk, tk)
            out = []
            for g in range(n_grp):
                m, acc = carry[g]
                k = kv_ref[pl.ds(ks, tk), g * LANES:(g + 1) * LANES]
                v = kv_ref[pl.ds(ks, tk), B_W + g * LANES:B_W + (g + 1) * LANES]
                s = _dot_nt(q2s[g], k)
                m_new = jnp.maximum(m, jnp.max(s, axis=1, keepdims=True))
                alpha = jnp.exp2(m - m_new)
                p = jnp.exp2(s - m_new)
                acc = alpha * acc + jnp.dot(p.astype(_BF16), v, preferred_element_type=_F32)
                out.append((m_new, acc))
            return tuple(out)

        init = tuple((jnp.full((2 * tq, 1), NEG_INF, _F32), jnp.zeros((2 * tq, LANES), _F32))
                     for _ in range(n_grp))
        final = body(0, init) if n_kv == 1 else lax.fori_loop(0, n_kv, body, init)
        tiles = []
        for g in range(n_grp):
            acc = final[g][1]
            head0, head1 = acc[:tq], acc[tq:]
            tiles.append(jnp.where(_lane_lo((tq, LANES)), head0 / pltpu.roll(head0, HEAD_DIM, axis=1),
                                   pltpu.roll(head1, HEAD_DIM, axis=1) / head1))
        yb = jnp.concatenate(tiles, axis=1)
        y_ref[rows, :] = _gated_norm(yb, z_ref[rows, :], g_ref[...]).astype(_BF16)


def _attn_b(bq, bkv, bz, gain, batch, seq):
    T = bq.shape[0]
    tq = min(B_TQ, seq)
    tk = min(B_TK, seq)
    step_rows = tq * (2 if seq == tk and seq % (2 * tq) == 0 else 1)
    nq = seq // step_rows
    kern = functools.partial(_attn_b_kernel, seq=seq, tk=tk, tq=tq)
    tok = lambda b, t: (b * nq + t, 0)
    return pl.pallas_call(
        kern,
        grid=(batch, nq),
        in_specs=[
            pl.BlockSpec((step_rows, B_W), tok),
            pl.BlockSpec((seq, 2 * B_W), lambda b, t: (b, 0)),
            pl.BlockSpec((step_rows, B_W), tok),
            pl.BlockSpec((1, B_W), lambda b, t: (0, 0)),
        ],
        out_specs=pl.BlockSpec((step_rows, B_W), tok),
        out_shape=jax.ShapeDtypeStruct((T, B_W), _BF16),
        compiler_params=_cparams(2),
        name="attn_b",
    )(bq, bkv, bz, gain)


def _c_tile_types():
    half = NA_ROWS // 2
    first = dict(ro_base=NA_ROWS - 1, start=[0] * C_ROWS)
    inner = dict(ro_base=NA_ROWS - 1 - half, start=list(range(C_ROWS)))
    last = dict(ro_base=NA_ROWS - 1 - (C_WIN - C_ROWS), start=[C_WIN - NA_ROWS] * C_ROWS)
    return (first, inner, last)


def _bias_kernel(rb_ref, out_ref):
    h = pl.program_id(0)
    n_ro = 2 * NA_ROWS - 1
    n_co = 2 * NA_COLS - 1
    qi = lax.broadcasted_iota(jnp.int32, (GRID_W, LANES), 0)
    lane = lax.broadcasted_iota(jnp.int32, (GRID_W, LANES), 1)
    kc = lane % GRID_W
    d = kc - qi
    cs = jnp.clip(qi - NA_COLS // 2, 0, GRID_W - NA_COLS)
    col_valid = (kc >= cs) & (kc < cs + NA_COLS)
    hits = [d == co - (NA_COLS - 1) for co in range(n_co)]
    neg = jnp.full((GRID_W, LANES), NEG_INF, _F32)
    rows = []
    for ro in range(n_ro):
        e = neg
        for co in range(n_co):
            e = jnp.where(hits[co], rb_ref[(h * n_ro + ro) * n_co + co] * LOG2E, e)
        rows.append(jnp.where(col_valid, e, neg))
    lo = lane < GRID_W
    for t, spec in enumerate(_c_tile_types()):
        for j in range(C_ROWS):
            def blk(kk):
                ok = spec["start"][j] <= kk < spec["start"][j] + NA_ROWS
                return rows[kk - j + spec["ro_base"]] if ok else neg
            tiles = [jnp.where(lo, blk(2 * i), blk(2 * i + 1)) for i in range(C_WIN // 2)]
            out_ref[t, j * GRID_W:(j + 1) * GRID_W, :] = jnp.concatenate(tiles, axis=1)


def _bias_table(rel_bias):
    flat = rel_bias.reshape(-1)
    return pl.pallas_call(
        _bias_kernel,
        grid=(C_HEADS,),
        in_specs=[pl.BlockSpec(memory_space=pltpu.SMEM)],
        out_specs=pl.BlockSpec((3, None, C_TQ, C_NK), lambda h: (0, h, 0, 0)),
        out_shape=jax.ShapeDtypeStruct((3, C_HEADS, C_TQ, C_NK), _F32),
        compiler_params=_cparams(1),
        name="c_bias",
    )(flat)


def _attn_c_kernel(q_ref, k_ref, v_ref, b_ref, z_ref, g_ref, y_ref, *, grid_rows):
    n_tiles = grid_rows // C_ROWS
    for h in range(C_STEP_TILES):
        tile = pl.program_id(1) * C_STEP_TILES + h
        kind = jnp.where(tile == 0, 0, jnp.where(tile == n_tiles - 1, 2, 1))
        r0 = tile * C_ROWS
        ws = jnp.clip(r0 - NA_ROWS // 2, 0, grid_rows - C_WIN)
        ks = pl.multiple_of(ws * GRID_W, GRID_W)
        rows = slice(h * C_TQ, (h + 1) * C_TQ)
        tiles = []
        for p in range(C_W // LANES):
            cols = slice(p * LANES, (p + 1) * LANES)
            ol, _ = _softmax_pv(_stack_heads(q_ref[rows, cols]), k_ref[pl.ds(ks, C_NK), cols],
                                v_ref[pl.ds(ks, C_NK), cols], b_ref[kind, p])
            tiles.append(_finish_heads(ol))
        yc = jnp.concatenate(tiles, axis=1)
        y_ref[rows, :] = _gated_norm(yc, z_ref[rows, :], g_ref[...]).astype(_BF16)


def _attn_c(cqkv, bias, cz, gain, batch, seq):
    T = cqkv.shape[0]
    grid_rows = seq // GRID_W
    n_tiles = seq // C_TQ
    assert grid_rows >= C_WIN and n_tiles % C_STEP_TILES == 0
    nq = n_tiles // C_STEP_TILES
    tq = C_STEP_TILES * C_TQ
    n_pair = C_W // LANES
    bias = bias.reshape(3, n_pair, 2 * C_TQ, C_NK)

    bias_spec = pl.BlockSpec((3, n_pair, 2 * C_TQ, C_NK), lambda b, t: (0, 0, 0, 0), pipeline_mode=pl.Buffered(1))
    kern = functools.partial(_attn_c_kernel, grid_rows=grid_rows)
    tok = lambda b, t: (b * nq + t, 0)
    resident = lambda col: pl.BlockSpec((seq, C_W), lambda b, t: (b, col), pipeline_mode=pl.Buffered(1))
    return pl.pallas_call(
        kern,
        grid=(batch, nq),
        in_specs=[pl.BlockSpec((tq, C_W), tok), resident(1), resident(2), bias_spec]
        + [pl.BlockSpec((tq, C_W), tok), pl.BlockSpec((1, C_W), lambda b, t: (0, 0))],
        out_specs=pl.BlockSpec((tq, C_W), tok),
        out_shape=jax.ShapeDtypeStruct((T, C_W), _BF16),
        compiler_params=_cparams(2),
        name="attn_c",
    )(cqkv, cqkv, cqkv, bias, cz, gain)


def _outproj_kernel(ya_ref, yb_ref, yc_ref, w_ref, npost_ref, x_ref, out_ref):
    y = jnp.concatenate([ya_ref[...], yb_ref[...], yc_ref[...]], axis=1)
    t = jnp.dot(y, w_ref[...], preferred_element_type=_F32)
    r = lax.rsqrt(jnp.mean(t * t, axis=-1, keepdims=True) + EPS)
    out_ref[...] = x_ref[...] + t * r * npost_ref[...]


def _outproj(ya, yb, yc, w, npost, x2d):
    T = x2d.shape[0]
    tm = min(OUT_ROW_TILE, T)
    row = lambda i: (i, 0)
    const = lambda i: (0, 0)
    return pl.pallas_call(
        _outproj_kernel,
        grid=(T // tm,),
        in_specs=[pl.BlockSpec((tm, A_W), row), pl.BlockSpec((tm, B_W), row), pl.BlockSpec((tm, C_W), row),
                  pl.BlockSpec((MIX_W, D_MODEL), const, pipeline_mode=pl.Buffered(1)),
                  pl.BlockSpec((1, D_MODEL), const), pl.BlockSpec((tm, D_MODEL), row)],
        out_specs=pl.BlockSpec((tm, D_MODEL), row),
        out_shape=jax.ShapeDtypeStruct((T, D_MODEL), _F32),
        compiler_params=_cparams(1),
        name="outproj",
    )(ya, yb, yc, w, npost, x2d)


def _lane_freqs(freqs, n_rot, width):
    lane = np.arange(LANES) % width
    idx = lane % (n_rot // 2)
    rot = lane < n_rot
    first = rot & (lane % n_rot < n_rot // 2)
    second = rot & ~first
    return jnp.where(rot, freqs[idx], 0.0), first, second


def _rotary_tables(ang, first, second):
    cos, sin = jnp.cos(ang), jnp.sin(ang)
    zero = jnp.zeros_like(sin)
    return jnp.stack([cos, jnp.where(first, -sin, zero), jnp.where(second, sin, zero)])


def _rope_tables(seq):
    freqs = ROPE_THETA ** (-jnp.arange(0, ROPE_DIMS, 2, dtype=_F32) / ROPE_DIMS)
    lane_f, first, second = _lane_freqs(freqs, ROPE_DIMS, HEAD_DIM)
    ang = jnp.arange(seq, dtype=_F32)[:, None] * lane_f[None, :]
    return _rotary_tables(ang, first, second)


def _axial_tables(seq):
    half = HEAD_DIM // 2
    freqs = AXIAL_THETA ** (-jnp.arange(0, half, 2, dtype=_F32) / half)
    lane_f, first, second = _lane_freqs(freqs, half, half)
    t = jnp.arange(seq)
    by_row = (np.arange(LANES) % HEAD_DIM) < half
    pos = jnp.where(by_row[None, :], (t // GRID_W).astype(_F32)[:, None], (t % GRID_W).astype(_F32)[:, None])
    return _rotary_tables(pos * lane_f[None, :], first, second)


def _layer(x2d, batch, seq, tabs, norm_pre, w_in, gq, gk, bd, bias, gains, w_out, npost):
    tab_a, tab_b = tabs
    ga, gb, gc = gains
    a1, a4, a16, cqkv, bq, bkv, az, bz, cz = _inproj(x2d, seq, norm_pre, w_in, tab_a, tab_b, gq, gk, bd)
    radii = {dil: window // (2 * dil) for window, dil in A_PATTERNS}
    o4, l4 = _attn_a_dil(a4, batch, seq, 4, radii[4])
    o16, l16 = _attn_a_dil(a16, batch, seq, 16, radii[16])
    ya = _attn_a_final(a1, o4, l4, o16, l16, az, ga, batch, seq, radii[1])
    yb = _attn_b(bq, bkv, bz, gb, batch, seq)
    yc = _attn_c(cqkv, bias, cz, gc, batch, seq)
    return _outproj(ya, yb, yc, w_out, npost, x2d)


def kernel(x_prompt, x_sample, norm_pre, w_in, q_norm, k_norm, rel_bias, branch_gain, w_out, norm_post):
    depth = w_in.shape[0]
    w_in_p = w_in.astype(_BF16)
    w_out_p = w_out.astype(_BF16)
    bd = jnp.asarray(np.kron(np.eye(B_W // HEAD_DIM), np.full((HEAD_DIM, HEAD_DIM), 1.0 / HEAD_DIM)), _BF16)
    groups = []
    for x in (x_prompt, x_sample):
        batch, seq, _ = x.shape
        groups.append(dict(y=x.reshape(batch * seq, D_MODEL), batch=batch, seq=seq,
                           tabs=(_rope_tables(seq), _axial_tables(seq))))
    for l in range(depth):
        bias = _bias_table(rel_bias[l])
        gq = jnp.tile(q_norm[l][None, :], (1, LANES // HEAD_DIM))
        gk = jnp.tile(k_norm[l][None, :], (1, LANES // HEAD_DIM))
        gain = branch_gain[l][None, :]
        gains = (gain[:, 0:A_W], gain[:, A_W:A_W + B_W], gain[:, A_W + B_W:MIX_W])
        for g in groups:
            g["y"] = _layer(g["y"], g["batch"], g["seq"], g["tabs"], norm_pre[l][None, :], w_in_p[l],
                            gq, gk, bd, bias, gains, w_out_p[l], norm_post[l][None, :])
    return tuple(g["y"].reshape(g["batch"], g["seq"], D_MODEL) for g in groups)
```

```python
import functools
import math

import jax
import jax.numpy as jnp
import numpy as np
from jax import lax
from jax.experimental import pallas as pl
from jax.experimental.pallas import tpu as pltpu

D_MODEL = 1024
HEAD_DIM = 64
GRID_W = 64
EPS = 1e-6
A_HEADS = 6
A_PATTERNS = ((128, 1), (512, 4), (2048, 16))
ROPE_THETA = 500000.0
ROPE_DIMS = HEAD_DIM // 4
B_HEADS = 4
B_KV_HEADS = 2
AXIAL_THETA = 10000.0
C_HEADS = 6
NA_ROWS = 8
NA_COLS = 16

A_W = A_HEADS * HEAD_DIM
B_W = B_HEADS * HEAD_DIM
B_KV_W = B_KV_HEADS * HEAD_DIM
C_W = C_HEADS * HEAD_DIM
MIX_W = A_W + B_W + C_W
QKV_W = 3 * A_W
OFF_A = 0
OFF_AZ = OFF_A + QKV_W
OFF_B = OFF_AZ + A_W
OFF_BZ = OFF_B + B_W + 2 * B_KV_W
OFF_C = OFF_BZ + B_W
OFF_CZ = OFF_C + QKV_W

LANES = 128
A_TILES = A_W // LANES
Q_SCALE = HEAD_DIM ** -0.5
LOG2E = math.log2(math.e)
VMEM_LIMIT = 56 * 1024 * 1024

IN_ROW_TILE = 512
OUT_ROW_TILE = 1024
A_SUB = 128
A_STEP_TOKENS = 2048
A1_Q_BLOCK = 1024
B_TQ = 512
B_TK = 2048
C_ROWS = 4
C_WIN = 12
C_TQ = C_ROWS * GRID_W
C_STEP_TILES = 8
C_NK = C_WIN * GRID_W
C_KV_PREFETCH_BYTES = 2 * 1024 * 1024
NEG_INF = float("-inf")

_F32 = jnp.float32
_BF16 = jnp.bfloat16


def _cparams(n_grid):
    return pltpu.CompilerParams(dimension_semantics=("arbitrary",) * n_grid,
                                vmem_limit_bytes=VMEM_LIMIT)


def _lane_lo(shape):
    return lax.broadcasted_iota(jnp.int32, shape, len(shape) - 1) % LANES < HEAD_DIM


def _stack_heads(q):
    lo = _lane_lo(q.shape)
    zero = jnp.zeros_like(q)
    return jnp.concatenate([jnp.where(lo, q, zero), jnp.where(lo, zero, q)], axis=0)


def _dot_nt(a, b):
    return lax.dot_general(a, b, (((1,), (1,)), ((), ())), preferred_element_type=_F32)


def _gated_norm(y, z, gain):
    hz = 0.5 * z.astype(_F32)
    u = y * (hz + hz * jnp.tanh(hz))
    r = lax.rsqrt(jnp.mean(u * u, axis=-1, keepdims=True) + EPS)
    return u * r * gain


def _rotate_tile(t, tab_ref, shift):
    up = pltpu.roll(t, LANES - shift, axis=1)
    dn = pltpu.roll(t, shift, axis=1)
    return t * tab_ref[0] + up * tab_ref[1] + dn * tab_ref[2]


def _inproj_kernel(x_ref, g_ref, w_ref, ta_ref, tb_ref, gq_ref, gk_ref, bd_ref,
                   a1_ref, a4_ref, a16_ref, c_ref, bq_ref, bkv_ref, az_ref, bz_ref, cz_ref, scr_ref):
    x = x_ref[...]
    tm = x.shape[0]
    r = lax.rsqrt(jnp.mean(x * x, axis=-1, keepdims=True) + EPS)
    h = (x * r * g_ref[...]).astype(_BF16)
    p = jnp.dot(h, w_ref[...], preferred_element_type=_F32)

    for j in range(QKV_W // LANES):
        t = p[:, OFF_A + j * LANES:OFF_A + (j + 1) * LANES]
        if j < 2 * A_TILES:
            t = _rotate_tile(t, ta_ref, ROPE_DIMS // 2)
        if j < A_TILES:
            t = t * (Q_SCALE * LOG2E)
        a1_ref[:, j * LANES:(j + 1) * LANES] = t.astype(_BF16)
        scr_ref[j] = t
    for dil, ref in ((4, a4_ref), (16, a16_ref)):
        for res in range(dil):
            for j in range(QKV_W // LANES):
                rows = scr_ref[j, pl.ds(res, tm // dil, stride=dil), :]
                ref[res, :, j * LANES:(j + 1) * LANES] = rows.astype(_BF16)

    c_ref[:, 0:C_W] = (p[:, OFF_C:OFF_C + C_W] * (Q_SCALE * LOG2E)).astype(_BF16)
    c_ref[:, C_W:QKV_W] = p[:, OFF_C + C_W:OFF_C + QKV_W].astype(_BF16)

    bd = bd_ref[...]

    def headnorm(t, gain):
        width = t.shape[1]
        sq = t * t
        hi = sq.astype(_BF16)
        lo = (sq - hi.astype(_F32)).astype(_BF16)
        ms = (jnp.dot(hi, bd[:width, :width], preferred_element_type=_F32)
              + jnp.dot(lo, bd[:width, :width], preferred_element_type=_F32))
        return t * lax.rsqrt(ms + EPS) * gain

    qn = headnorm(p[:, OFF_B:OFF_B + B_W], jnp.concatenate([gq_ref[...]] * (B_W // LANES), axis=1))
    for j in range(B_W // LANES):
        t = _rotate_tile(qn[:, j * LANES:(j + 1) * LANES], tb_ref, HEAD_DIM // 4) * (Q_SCALE * LOG2E)
        bq_ref[:, j * LANES:(j + 1) * LANES] = t.astype(_BF16)
    lo_half = _lane_lo((tm, LANES))
    kn = _rotate_tile(headnorm(p[:, OFF_B + B_W:OFF_B + B_W + B_KV_W], gk_ref[...]), tb_ref, HEAD_DIM // 4)
    k_sw = pltpu.roll(kn, HEAD_DIM, axis=1)
    bkv_ref[:, 0:LANES] = jnp.where(lo_half, kn, k_sw).astype(_BF16)
    bkv_ref[:, LANES:2 * LANES] = jnp.where(lo_half, k_sw, kn).astype(_BF16)
    vv = p[:, OFF_B + B_W + B_KV_W:OFF_B + B_W + 2 * B_KV_W]
    bkv_ref[:, B_W:B_W + LANES] = jnp.where(lo_half, vv, 1.0).astype(_BF16)
    bkv_ref[:, B_W + LANES:2 * B_W] = jnp.where(lo_half, pltpu.roll(vv, HEAD_DIM, axis=1), 1.0).astype(_BF16)

    az_ref[...] = p[:, OFF_AZ:OFF_AZ + A_W].astype(_BF16)
    bz_ref[...] = p[:, OFF_BZ:OFF_BZ + B_W].astype(_BF16)
    cz_ref[...] = p[:, OFF_CZ:OFF_CZ + C_W].astype(_BF16)


def _inproj(x2d, seq, norm_pre, w, tab_a, tab_b, gq, gk, bd):
    T = x2d.shape[0]
    tm = min(IN_ROW_TILE, seq)
    n_seq = seq // tm
    nw = w.shape[1]
    const = lambda i: (0, 0)
    row = lambda i: (i, 0)
    row3 = lambda i: (0, i, 0)
    tab = lambda i: (0, i % n_seq, 0)
    out_shape = (
        jax.ShapeDtypeStruct((T, QKV_W), _BF16),
        jax.ShapeDtypeStruct((4, T // 4, QKV_W), _BF16),
        jax.ShapeDtypeStruct((16, T // 16, QKV_W), _BF16),
        jax.ShapeDtypeStruct((T, QKV_W), _BF16),
        jax.ShapeDtypeStruct((T, B_W), _BF16),
        jax.ShapeDtypeStruct((T, 2 * B_W), _BF16),
        jax.ShapeDtypeStruct((T, A_W), _BF16),
        jax.ShapeDtypeStruct((T, B_W), _BF16),
        jax.ShapeDtypeStruct((T, C_W), _BF16),
    )
    out_specs = (
        pl.BlockSpec((tm, QKV_W), row),
        pl.BlockSpec((4, tm // 4, QKV_W), row3),
        pl.BlockSpec((16, tm // 16, QKV_W), row3),
        pl.BlockSpec((tm, QKV_W), row),
        pl.BlockSpec((tm, B_W), row),
        pl.BlockSpec((tm, 2 * B_W), row),
        pl.BlockSpec((tm, A_W), row),
        pl.BlockSpec((tm, B_W), row),
        pl.BlockSpec((tm, C_W), row),
    )
    in_specs = [
        pl.BlockSpec((tm, D_MODEL), row),
        pl.BlockSpec((1, D_MODEL), const),
        pl.BlockSpec((D_MODEL, nw), const, pipeline_mode=pl.Buffered(1)),
        pl.BlockSpec((3, tm, LANES), tab),
        pl.BlockSpec((3, tm, LANES), tab),
        pl.BlockSpec((1, LANES), const),
        pl.BlockSpec((1, LANES), const),
        pl.BlockSpec((B_W, B_W), const),
    ]
    return pl.pallas_call(
        _inproj_kernel,
        grid=(T // tm,),
        in_specs=in_specs,
        out_specs=out_specs,
        out_shape=out_shape,
        scratch_shapes=[pltpu.VMEM((QKV_W // LANES, tm, LANES), _F32)],
        compiler_params=_cparams(1),
        name="inproj",
    )(x2d, norm_pre, w, tab_a, tab_b, gq, gk, bd)


def _softmax_pv(q2, kp, vp, bias):
    s = _dot_nt(q2, kp) + bias
    m = jnp.max(s, axis=1, keepdims=True)
    e = jnp.exp2(s - m).astype(_BF16)
    return jnp.dot(e, jnp.concatenate([vp, jnp.ones_like(vp)], axis=1), preferred_element_type=_F32), m


def _unstack_heads(ol, m):
    n = ol.shape[0] // 2
    lo = _lane_lo((n, LANES))
    num = jnp.where(lo, ol[:n, :LANES], ol[n:, :LANES])
    den = jnp.where(lo, ol[:n, LANES:], ol[n:, LANES:])
    return num, den, jnp.where(lo, m[:n], m[n:])


def _finish_heads(ol, m):
    num, den, mx = _unstack_heads(ol, m)
    return num / den, mx + jnp.log2(den)


def _band_tile(q, k_ref, v_ref, kidx, kstart, nk, cols, bias):
    kp = k_ref[kidx + (pl.ds(kstart, nk), cols)]
    vp = v_ref[kidx + (pl.ds(kstart, nk), cols)]
    return _finish_heads(*_softmax_pv(_stack_heads(q), kp, vp, bias))


def _band_delta(sub, nk):
    row = lax.broadcasted_iota(jnp.int32, (2 * sub, nk), 0) % sub
    col = lax.broadcasted_iota(jnp.int32, (2 * sub, nk), 1)
    return col - row


def _band_bias(delta, shift, radius):
    rel = delta + shift
    return jnp.where((rel >= -radius) & (rel <= radius), 0.0, NEG_INF).astype(_F32)


def _attn_a_dil_kernel(q_ref, k_ref, v_ref, o_ref, lse_ref, *, dil, q_block, seq_len, radius, sub):
    nk = min(sub + 2 * radius, seq_len)
    base = pl.program_id(1) * q_block
    delta = _band_delta(sub, nk)
    windows = []
    for si in range(q_block // sub):
        l0 = base + si * sub
        kstart = pl.multiple_of(jnp.clip(l0 - radius, 0, seq_len - nk), HEAD_DIM)
        windows.append((kstart, _band_bias(delta, kstart - l0, radius)))

    def body(res, carry):
        for si, (kstart, bias) in enumerate(windows):
            for p in range(A_TILES):
                cols = slice(p * LANES, (p + 1) * LANES)
                q = q_ref[res, si * sub:(si + 1) * sub, cols]
                o, lse = _band_tile(q, k_ref, v_ref, (res,), kstart, nk, cols, bias)
                rows = pl.ds(si * sub * dil + res, sub, stride=dil)
                o_ref[p, rows, :] = o
                lse_ref[p, rows, :] = lse
        return carry

    def pair_body(rp, carry):
        for p in range(A_TILES):
            cols = slice(p * LANES, (p + 1) * LANES)
            res = (2 * rp, 2 * rp + 1)
            q2 = jnp.concatenate([_stack_heads(q_ref[r, :, cols]) for r in res], axis=0)
            kp = jnp.concatenate([k_ref[r, :, cols] for r in res], axis=0)
            vp = jnp.concatenate([v_ref[r, :, cols] for r in res], axis=0)
            ol, m = _softmax_pv(q2, kp, vp, pair_bias)
            for i, r in enumerate(res):
                blk = slice(i * 2 * sub, (i + 1) * 2 * sub)
                rows = pl.ds(r, sub, stride=dil)
                o_ref[p, rows, :], lse_ref[p, rows, :] = _finish_heads(ol[blk], m[blk])
        return carry

    if seq_len == q_block == sub and 2 * nk <= 2 * LANES and dil % 2 == 0:
        bias = windows[0][1]
        neg = jnp.full_like(bias, NEG_INF)
        pair_bias = jnp.concatenate([jnp.concatenate([bias, neg], axis=1),
                                     jnp.concatenate([neg, bias], axis=1)], axis=0)
        lax.fori_loop(0, dil // 2, pair_body, 0, unroll=min(dil // 2, 4))
    else:
        lax.fori_loop(0, dil, body, 0, unroll=max(1, min(dil, 8 * sub // q_block)))


def _attn_a_dil(a_res, batch, seq, dil, radius):
    L = seq // dil
    tl = min(max(A_STEP_TOKENS // dil, A_SUB), L)
    sub = min(A_SUB, tl)
    n_lt = L // tl
    T = batch * seq
    kern = functools.partial(_attn_a_dil_kernel, dil=dil, q_block=tl, seq_len=L, radius=radius, sub=sub)
    q_spec = pl.BlockSpec((dil, tl, A_W), lambda b, t: (0, b * n_lt + t, 0))
    k_spec = pl.BlockSpec((dil, L, A_W), lambda b, t: (0, b, 1))
    v_spec = pl.BlockSpec((dil, L, A_W), lambda b, t: (0, b, 2))
    o_spec = pl.BlockSpec((A_TILES, tl * dil, LANES), lambda b, t: (0, b * n_lt + t, 0))
    out_shape = (jax.ShapeDtypeStruct((A_TILES, T, LANES), _F32),) * 2
    return pl.pallas_call(
        kern,
        grid=(batch, n_lt),
        in_specs=[q_spec, k_spec, v_spec],
        out_specs=(o_spec, o_spec),
        out_shape=out_shape,
        compiler_params=_cparams(2),
        name=f"attn_a_d{dil}",
    )(a_res, a_res, a_res)


def _attn_a_final_kernel(q_ref, k_ref, v_ref, o4_ref, l4_ref, o16_ref, l16_ref, z_ref, g_ref, y_ref,
                         *, q_block, seq_len, radius, sub):
    nk = min(sub + 2 * radius, seq_len)
    n_sub = q_block // sub
    base = pl.program_id(1) * q_block
    delta = _band_delta(sub, nk)
    inner_bias = _band_bias(delta, -radius, radius)
    for si in range(n_sub):
        l0 = base + si * sub
        kstart = pl.multiple_of(jnp.clip(l0 - radius, 0, seq_len - nk), HEAD_DIM)
        inner = nk == sub + 2 * radius and 0 < si < n_sub - 1
        bias = inner_bias if inner else _band_bias(delta, kstart - l0, radius)
        rows = slice(si * sub, (si + 1) * sub)
        tiles = []
        for p in range(A_TILES):
            cols = slice(p * LANES, (p + 1) * LANES)
            kp = k_ref[pl.ds(kstart, nk), cols]
            vp = v_ref[pl.ds(kstart, nk), cols]
            num1, den1, m1 = _unstack_heads(*_softmax_pv(_stack_heads(q_ref[rows, cols]), kp, vp, bias))
            l4, l16 = l4_ref[p, rows, :], l16_ref[p, rows, :]
            m = jnp.maximum(jnp.maximum(m1, l4), l16)
            w1, e4, e16 = jnp.exp2(m1 - m), jnp.exp2(l4 - m), jnp.exp2(l16 - m)
            mix = w1 * num1 + e4 * o4_ref[p, rows, :] + e16 * o16_ref[p, rows, :]
            tiles.append(mix / (w1 * den1 + e4 + e16))
        ya = jnp.concatenate(tiles, axis=1)
        y_ref[rows, :] = _gated_norm(ya, z_ref[rows, :], g_ref[...]).astype(_BF16)


def _attn_a_final(a1, o4, l4, o16, l16, az, gain, batch, seq, radius):
    T = a1.shape[0]
    tl = min(A1_Q_BLOCK, seq)
    sub = min(A_SUB, tl)
    n_lt = seq // tl
    kern = functools.partial(_attn_a_final_kernel, q_block=tl, seq_len=seq, radius=radius, sub=sub)
    tok = lambda b, t: (b * n_lt + t, 0)
    tok3 = lambda b, t: (0, b * n_lt + t, 0)
    st_spec = pl.BlockSpec((A_TILES, tl, LANES), tok3)
    return pl.pallas_call(
        kern,
        grid=(batch, n_lt),
        in_specs=[
            pl.BlockSpec((tl, A_W), tok),
            pl.BlockSpec((seq, A_W), lambda b, t: (b, 1)),
            pl.BlockSpec((seq, A_W), lambda b, t: (b, 2)),
            st_spec, st_spec, st_spec, st_spec,
            pl.BlockSpec((tl, A_W), tok),
            pl.BlockSpec((1, A_W), lambda b, t: (0, 0)),
        ],
        out_specs=pl.BlockSpec((tl, A_W), tok),
        out_shape=jax.ShapeDtypeStruct((T, A_W), _BF16),
        compiler_params=_cparams(2),
        name="attn_a_d1",
    )(a1, a1, a1, o4, l4, o16, l16, az, gain)


def _attn_b_kernel(q_ref, kv_ref, z_ref, g_ref, y_ref, *, seq, tk, tq):
    n_grp = B_W // LANES
    n_kv = seq // tk
    for h in range(q_ref.shape[0] // tq):
        rows = slice(h * tq, (h + 1) * tq)
        q2s = [_stack_heads(q_ref[rows, g * LANES:(g + 1) * LANES]) for g in range(n_grp)]

        def body(kt, carry):
            ks = pl.multiple_of(kt * tk, tk)
            out = []
            for g in range(n_grp):
                m, acc = carry[g]
                k = kv_ref[pl.ds(ks, tk), g * LANES:(g + 1) * LANES]
                v = kv_ref[pl.ds(ks, tk), B_W + g * LANES:B_W + (g + 1) * LANES]
                s = _dot_nt(q2s[g], k)
                m_new = jnp.maximum(m, jnp.max(s, axis=1, keepdims=True))
                alpha = jnp.exp2(m - m_new)
                p = jnp.exp2(s - m_new)
                acc = alpha * acc + jnp.dot(p.astype(_BF16), v, preferred_element_type=_F32)
                out.append((m_new, acc))
            return tuple(out)

        init = tuple((jnp.full((2 * tq, 1), NEG_INF, _F32), jnp.zeros((2 * tq, LANES), _F32))
                     for _ in range(n_grp))
        final = body(0, init) if n_kv == 1 else lax.fori_loop(0, n_kv, body, init)
        tiles = []
        for g in range(n_grp):
            acc = final[g][1]
            head0, head1 = acc[:tq], acc[tq:]
            tiles.append(jnp.where(_lane_lo((tq, LANES)), head0 / pltpu.roll(head0, HEAD_DIM, axis=1),
                                   pltpu.roll(head1, HEAD_DIM, axis=1) / head1))
        yb = jnp.concatenate(tiles, axis=1)
        y_ref[rows, :] = _gated_norm(yb, z_ref[rows, :], g_ref[...]).astype(_BF16)


def _attn_b(bq, bkv, bz, gain, batch, seq):
    T = bq.shape[0]
    tq = min(B_TQ, seq)
    tk = min(B_TK, seq)
    step_rows = tq * (2 if seq == tk and seq % (2 * tq) == 0 else 1)
    nq = seq // step_rows
    kern = functools.partial(_attn_b_kernel, seq=seq, tk=tk, tq=tq)
    tok = lambda b, t: (b * nq + t, 0)
    return pl.pallas_call(
        kern,
        grid=(batch, nq),
        in_specs=[
            pl.BlockSpec((step_rows, B_W), tok),
            pl.BlockSpec((seq, 2 * B_W), lambda b, t: (b, 0)),
            pl.BlockSpec((step_rows, B_W), tok),
            pl.BlockSpec((1, B_W), lambda b, t: (0, 0)),
        ],
        out_specs=pl.BlockSpec((step_rows, B_W), tok),
        out_shape=jax.ShapeDtypeStruct((T, B_W), _BF16),
        compiler_params=_cparams(2),
        name="attn_b",
    )(bq, bkv, bz, gain)


def _c_tile_types():
    half = NA_ROWS // 2
    first = dict(ro_base=NA_ROWS - 1, start=[0] * C_ROWS)
    inner = dict(ro_base=NA_ROWS - 1 - half, start=list(range(C_ROWS)))
    last = dict(ro_base=NA_ROWS - 1 - (C_WIN - C_ROWS), start=[C_WIN - NA_ROWS] * C_ROWS)
    return (first, inner, last)


def _bias_kernel(rb_ref, out_ref):
    h = pl.program_id(0)
    n_ro = 2 * NA_ROWS - 1
    n_co = 2 * NA_COLS - 1
    qi = lax.broadcasted_iota(jnp.int32, (GRID_W, LANES), 0)
    lane = lax.broadcasted_iota(jnp.int32, (GRID_W, LANES), 1)
    kc = lane % GRID_W
    d = kc - qi
    cs = jnp.clip(qi - NA_COLS // 2, 0, GRID_W - NA_COLS)
    col_valid = (kc >= cs) & (kc < cs + NA_COLS)
    hits = [d == co - (NA_COLS - 1) for co in range(n_co)]
    neg = jnp.full((GRID_W, LANES), NEG_INF, _F32)
    rows = []
    for ro in range(n_ro):
        e = neg
        for co in range(n_co):
            e = jnp.where(hits[co], rb_ref[(h * n_ro + ro) * n_co + co] * LOG2E, e)
        rows.append(jnp.where(col_valid, e, neg))
    lo = lane < GRID_W
    for t, spec in enumerate(_c_tile_types()):
        for j in range(C_ROWS):
            def blk(kk):
                ok = spec["start"][j] <= kk < spec["start"][j] + NA_ROWS
                return rows[kk - j + spec["ro_base"]] if ok else neg
            tiles = [jnp.where(lo, blk(2 * i), blk(2 * i + 1)) for i in range(C_WIN // 2)]
            out_ref[t, j * GRID_W:(j + 1) * GRID_W, :] = jnp.concatenate(tiles, axis=1)


def _bias_table(rel_bias):
    flat = rel_bias.reshape(-1)
    return pl.pallas_call(
        _bias_kernel,
        grid=(C_HEADS,),
        in_specs=[pl.BlockSpec(memory_space=pltpu.SMEM)],
        out_specs=pl.BlockSpec((3, None, C_TQ, C_NK), lambda h: (0, h, 0, 0)),
        out_shape=jax.ShapeDtypeStruct((3, C_HEADS, C_TQ, C_NK), _F32),
        compiler_params=_cparams(1),
        name="c_bias",
    )(flat)


def _attn_c_kernel(q_ref, k_ref, v_ref, b_ref, z_ref, g_ref, y_ref, *, grid_rows):
    n_tiles = grid_rows // C_ROWS
    for h in range(C_STEP_TILES):
        tile = pl.program_id(1) * C_STEP_TILES + h
        kind = jnp.where(tile == 0, 0, jnp.where(tile == n_tiles - 1, 2, 1))
        r0 = tile * C_ROWS
        ws = jnp.clip(r0 - NA_ROWS // 2, 0, grid_rows - C_WIN)
        ks = pl.multiple_of(ws * GRID_W, GRID_W)
        rows = slice(h * C_TQ, (h + 1) * C_TQ)
        tiles = []
        for p in range(C_W // LANES):
            cols = slice(p * LANES, (p + 1) * LANES)
            num, den, _ = _unstack_heads(*_softmax_pv(_stack_heads(q_ref[rows, cols]), k_ref[pl.ds(ks, C_NK), cols],
                                                      v_ref[pl.ds(ks, C_NK), cols], b_ref[kind, p]))
            tiles.append(num / den)
        yc = jnp.concatenate(tiles, axis=1)
        y_ref[rows, :] = _gated_norm(yc, z_ref[rows, :], g_ref[...]).astype(_BF16)


def _attn_c(cqkv, bias, cz, gain, batch, seq):
    T = cqkv.shape[0]
    grid_rows = seq // GRID_W
    n_tiles = seq // C_TQ
    assert grid_rows >= C_WIN and n_tiles % C_STEP_TILES == 0
    nq = n_tiles // C_STEP_TILES
    tq = C_STEP_TILES * C_TQ
    n_pair = C_W // LANES
    bias = bias.reshape(3, n_pair, 2 * C_TQ, C_NK)

    bias_spec = pl.BlockSpec((3, n_pair, 2 * C_TQ, C_NK), lambda b, t: (0, 0, 0, 0), pipeline_mode=pl.Buffered(1))
    kern = functools.partial(_attn_c_kernel, grid_rows=grid_rows)
    tok = lambda b, t: (b * nq + t, 0)
    kv_buffers = 2 if seq * C_W * 2 <= C_KV_PREFETCH_BYTES else 1
    resident = lambda col: pl.BlockSpec((seq, C_W), lambda b, t: (b, col), pipeline_mode=pl.Buffered(kv_buffers))
    return pl.pallas_call(
        kern,
        grid=(batch, nq),
        in_specs=[pl.BlockSpec((tq, C_W), tok), resident(1), resident(2), bias_spec]
        + [pl.BlockSpec((tq, C_W), tok), pl.BlockSpec((1, C_W), lambda b, t: (0, 0))],
        out_specs=pl.BlockSpec((tq, C_W), tok),
        out_shape=jax.ShapeDtypeStruct((T, C_W), _BF16),
        compiler_params=_cparams(2),
        name="attn_c",
    )(cqkv, cqkv, cqkv, bias, cz, gain)


def _outproj_kernel(ya_ref, yb_ref, yc_ref, w_ref, npost_ref, x_ref, out_ref):
    y = jnp.concatenate([ya_ref[...], yb_ref[...], yc_ref[...]], axis=1)
    t = jnp.dot(y, w_ref[...], preferred_element_type=_F32)
    r = lax.rsqrt(jnp.mean(t * t, axis=-1, keepdims=True) + EPS)
    out_ref[...] = x_ref[...] + t * r * npost_ref[...]


def _outproj(ya, yb, yc, w, npost, x2d):
    T = x2d.shape[0]
    tm = min(OUT_ROW_TILE, T)
    row = lambda i: (i, 0)
    const = lambda i: (0, 0)
    return pl.pallas_call(
        _outproj_kernel,
        grid=(T // tm,),
        in_specs=[pl.BlockSpec((tm, A_W), row), pl.BlockSpec((tm, B_W), row), pl.BlockSpec((tm, C_W), row),
                  pl.BlockSpec((MIX_W, D_MODEL), const, pipeline_mode=pl.Buffered(1)),
                  pl.BlockSpec((1, D_MODEL), const), pl.BlockSpec((tm, D_MODEL), row)],
        out_specs=pl.BlockSpec((tm, D_MODEL), row),
        out_shape=jax.ShapeDtypeStruct((T, D_MODEL), _F32),
        compiler_params=_cparams(1),
        name="outproj",
    )(ya, yb, yc, w, npost, x2d)


def _lane_freqs(freqs, n_rot, width):
    lane = np.arange(LANES) % width
    idx = lane % (n_rot // 2)
    rot = lane < n_rot
    first = rot & (lane % n_rot < n_rot // 2)
    second = rot & ~first
    return jnp.where(rot, freqs[idx], 0.0), first, second


def _rotary_tables(ang, first, second):
    cos, sin = jnp.cos(ang), jnp.sin(ang)
    zero = jnp.zeros_like(sin)
    return jnp.stack([cos, jnp.where(first, -sin, zero), jnp.where(second, sin, zero)])


def _rope_tables(seq):
    freqs = ROPE_THETA ** (-jnp.arange(0, ROPE_DIMS, 2, dtype=_F32) / ROPE_DIMS)
    lane_f, first, second = _lane_freqs(freqs, ROPE_DIMS, HEAD_DIM)
    ang = jnp.arange(seq, dtype=_F32)[:, None] * lane_f[None, :]
    return _rotary_tables(ang, first, second)


def _axial_tables(seq):
    half = HEAD_DIM // 2
    freqs = AXIAL_THETA ** (-jnp.arange(0, half, 2, dtype=_F32) / half)
    lane_f, first, second = _lane_freqs(freqs, half, half)
    t = jnp.arange(seq)
    by_row = (np.arange(LANES) % HEAD_DIM) < half
    pos = jnp.where(by_row[None, :], (t // GRID_W).astype(_F32)[:, None], (t % GRID_W).astype(_F32)[:, None])
    return _rotary_tables(pos * lane_f[None, :], first, second)


def _layer(x2d, batch, seq, tabs, norm_pre, w_in, gq, gk, bd, bias, gains, w_out, npost):
    tab_a, tab_b = tabs
    ga, gb, gc = gains
    a1, a4, a16, cqkv, bq, bkv, az, bz, cz = _inproj(x2d, seq, norm_pre, w_in, tab_a, tab_b, gq, gk, bd)
    radii = {dil: window // (2 * dil) for window, dil in A_PATTERNS}
    o4, l4 = _attn_a_dil(a4, batch, seq, 4, radii[4])
    o16, l16 = _attn_a_dil(a16, batch, seq, 16, radii[16])
    ya = _attn_a_final(a1, o4, l4, o16, l16, az, ga, batch, seq, radii[1])
    yb = _attn_b(bq, bkv, bz, gb, batch, seq)
    yc = _attn_c(cqkv, bias, cz, gc, batch, seq)
    return _outproj(ya, yb, yc, w_out, npost, x2d)


def kernel(x_prompt, x_sample, norm_pre, w_in, q_norm, k_norm, rel_bias, branch_gain, w_out, norm_post):
    depth = w_in.shape[0]
    w_in_p = w_in.astype(_BF16)
    w_out_p = w_out.astype(_BF16)
    bd = jnp.asarray(np.kron(np.eye(B_W // HEAD_DIM), np.full((HEAD_DIM, HEAD_DIM), 1.0 / HEAD_DIM)), _BF16)
    groups = []
    for x in (x_prompt, x_sample):
        batch, seq, _ = x.shape
        groups.append(dict(y=x.reshape(batch * seq, D_MODEL), batch=batch, seq=seq,
                           tabs=(_rope_tables(seq), _axial_tables(seq))))
    for l in range(depth):
        bias = _bias_table(rel_bias[l])
        gq = jnp.tile(q_norm[l][None, :], (1, LANES // HEAD_DIM))
        gk = jnp.tile(k_norm[l][None, :], (1, LANES // HEAD_DIM))
        gain = branch_gain[l][None, :]
        gains = (gain[:, 0:A_W], gain[:, A_W:A_W + B_W], gain[:, A_W + B_W:MIX_W])
        for g in groups:
            g["y"] = _layer(g["y"], g["batch"], g["seq"], g["tabs"], norm_pre[l][None, :], w_in_p[l],
                            gq, gk, bd, bias, gains, w_out_p[l], norm_post[l][None, :])
    return tuple(g["y"].reshape(g["batch"], g["seq"], D_MODEL) for g in groups)
```

```python
import functools
import math

import jax
import jax.numpy as jnp
import numpy as np
from jax import lax
from jax.experimental import pallas as pl
from jax.experimental.pallas import tpu as pltpu

D_MODEL = 1024
HEAD_DIM = 64
GRID_W = 64
EPS = 1e-6
A_HEADS = 6
A_PATTERNS = ((128, 1), (512, 4), (2048, 16))
ROPE_THETA = 500000.0
ROPE_DIMS = HEAD_DIM // 4
B_HEADS = 4
B_KV_HEADS = 2
AXIAL_THETA = 10000.0
C_HEADS = 6
NA_ROWS = 8
NA_COLS = 16

A_W = A_HEADS * HEAD_DIM
B_W = B_HEADS * HEAD_DIM
B_KV_W = B_KV_HEADS * HEAD_DIM
C_W = C_HEADS * HEAD_DIM
MIX_W = A_W + B_W + C_W
QKV_W = 3 * A_W
OFF_A = 0
OFF_AZ = OFF_A + QKV_W
OFF_B = OFF_AZ + A_W
OFF_BZ = OFF_B + B_W + 2 * B_KV_W
OFF_C = OFF_BZ + B_W
OFF_CZ = OFF_C + QKV_W

LANES = 128
A_TILES = A_W // LANES
Q_SCALE = HEAD_DIM ** -0.5
LOG2E = math.log2(math.e)
VMEM_LIMIT = 56 * 1024 * 1024

IN_ROW_TILE = 512
OUT_ROW_TILE = 1024
A_SUB = 128
A_STEP_TOKENS = 2048
A1_Q_BLOCK = 1024
B_TQ = 512
B_TK = 2048
B_KV_UNROLL = 4
C_ROWS = 4
C_WIN = 12
C_TQ = C_ROWS * GRID_W
C_STEP_TILES = 8
C_NK = C_WIN * GRID_W
C_KV_PREFETCH_BYTES = 2 * 1024 * 1024
NEG_INF = float("-inf")

_F32 = jnp.float32
_BF16 = jnp.bfloat16


def _cparams(n_grid):
    return pltpu.CompilerParams(dimension_semantics=("arbitrary",) * n_grid,
                                vmem_limit_bytes=VMEM_LIMIT)


def _lane_lo(shape):
    return lax.broadcasted_iota(jnp.int32, shape, len(shape) - 1) % LANES < HEAD_DIM


def _stack_heads(q):
    lo = _lane_lo(q.shape)
    zero = jnp.zeros_like(q)
    return jnp.concatenate([jnp.where(lo, q, zero), jnp.where(lo, zero, q)], axis=0)


def _dot_nt(a, b):
    return lax.dot_general(a, b, (((1,), (1,)), ((), ())), preferred_element_type=_F32)


def _gated_norm(y, z, gain):
    hz = 0.5 * z.astype(_F32)
    u = y * (hz + hz * jnp.tanh(hz))
    r = lax.rsqrt(jnp.mean(u * u, axis=-1, keepdims=True) + EPS)
    return u * r * gain


def _rotate_tile(t, tab_ref, shift):
    up = pltpu.roll(t, LANES - shift, axis=1)
    dn = pltpu.roll(t, shift, axis=1)
    return t * tab_ref[0] + up * tab_ref[1] + dn * tab_ref[2]


def _inproj_kernel(x_ref, g_ref, w_ref, ta_ref, tb_ref, gq_ref, gk_ref, bd_ref,
                   a1_ref, a4_ref, a16_ref, c_ref, bq_ref, bkv_ref, az_ref, bz_ref, cz_ref, scr_ref):
    x = x_ref[...]
    tm = x.shape[0]
    r = lax.rsqrt(jnp.mean(x * x, axis=-1, keepdims=True) + EPS)
    h = (x * r * g_ref[...]).astype(_BF16)
    p = jnp.dot(h, w_ref[...], preferred_element_type=_F32)

    for j in range(QKV_W // LANES):
        t = p[:, OFF_A + j * LANES:OFF_A + (j + 1) * LANES]
        if j < 2 * A_TILES:
            t = _rotate_tile(t, ta_ref, ROPE_DIMS // 2)
        if j < A_TILES:
            t = t * (Q_SCALE * LOG2E)
        a1_ref[:, j * LANES:(j + 1) * LANES] = t.astype(_BF16)
        scr_ref[j] = t
    for dil, ref in ((4, a4_ref), (16, a16_ref)):
        for res in range(dil):
            for j in range(QKV_W // LANES):
                rows = scr_ref[j, pl.ds(res, tm // dil, stride=dil), :]
                ref[res, :, j * LANES:(j + 1) * LANES] = rows.astype(_BF16)

    c_ref[:, 0:C_W] = (p[:, OFF_C:OFF_C + C_W] * (Q_SCALE * LOG2E)).astype(_BF16)
    c_ref[:, C_W:QKV_W] = p[:, OFF_C + C_W:OFF_C + QKV_W].astype(_BF16)

    bd = bd_ref[...]

    def headnorm(t, gain):
        width = t.shape[1]
        sq = t * t
        hi = sq.astype(_BF16)
        lo = (sq - hi.astype(_F32)).astype(_BF16)
        ms = (jnp.dot(hi, bd[:width, :width], preferred_element_type=_F32)
              + jnp.dot(lo, bd[:width, :width], preferred_element_type=_F32))
        return t * lax.rsqrt(ms + EPS) * gain

    qn = headnorm(p[:, OFF_B:OFF_B + B_W], jnp.concatenate([gq_ref[...]] * (B_W // LANES), axis=1))
    for j in range(B_W // LANES):
        t = _rotate_tile(qn[:, j * LANES:(j + 1) * LANES], tb_ref, HEAD_DIM // 4) * (Q_SCALE * LOG2E)
        bq_ref[:, j * LANES:(j + 1) * LANES] = t.astype(_BF16)
    lo_half = _lane_lo((tm, LANES))
    kn = _rotate_tile(headnorm(p[:, OFF_B + B_W:OFF_B + B_W + B_KV_W], gk_ref[...]), tb_ref, HEAD_DIM // 4)
    k_sw = pltpu.roll(kn, HEAD_DIM, axis=1)
    bkv_ref[:, 0:LANES] = jnp.where(lo_half, kn, k_sw).astype(_BF16)
    bkv_ref[:, LANES:2 * LANES] = jnp.where(lo_half, k_sw, kn).astype(_BF16)
    vv = p[:, OFF_B + B_W + B_KV_W:OFF_B + B_W + 2 * B_KV_W]
    bkv_ref[:, B_W:B_W + LANES] = jnp.where(lo_half, vv, 1.0).astype(_BF16)
    bkv_ref[:, B_W + LANES:2 * B_W] = jnp.where(lo_half, pltpu.roll(vv, HEAD_DIM, axis=1), 1.0).astype(_BF16)

    az_ref[...] = p[:, OFF_AZ:OFF_AZ + A_W].astype(_BF16)
    bz_ref[...] = p[:, OFF_BZ:OFF_BZ + B_W].astype(_BF16)
    cz_ref[...] = p[:, OFF_CZ:OFF_CZ + C_W].astype(_BF16)


def _inproj(x2d, seq, norm_pre, w, tab_a, tab_b, gq, gk, bd):
    T = x2d.shape[0]
    tm = min(IN_ROW_TILE, seq)
    n_seq = seq // tm
    nw = w.shape[1]
    const = lambda i: (0, 0)
    row = lambda i: (i, 0)
    row3 = lambda i: (0, i, 0)
    tab = lambda i: (0, i % n_seq, 0)
    out_shape = (
        jax.ShapeDtypeStruct((T, QKV_W), _BF16),
        jax.ShapeDtypeStruct((4, T // 4, QKV_W), _BF16),
        jax.ShapeDtypeStruct((16, T // 16, QKV_W), _BF16),
        jax.ShapeDtypeStruct((T, QKV_W), _BF16),
        jax.ShapeDtypeStruct((T, B_W), _BF16),
        jax.ShapeDtypeStruct((T, 2 * B_W), _BF16),
        jax.ShapeDtypeStruct((T, A_W), _BF16),
        jax.ShapeDtypeStruct((T, B_W), _BF16),
        jax.ShapeDtypeStruct((T, C_W), _BF16),
    )
    out_specs = (
        pl.BlockSpec((tm, QKV_W), row),
        pl.BlockSpec((4, tm // 4, QKV_W), row3),
        pl.BlockSpec((16, tm // 16, QKV_W), row3),
        pl.BlockSpec((tm, QKV_W), row),
        pl.BlockSpec((tm, B_W), row),
        pl.BlockSpec((tm, 2 * B_W), row),
        pl.BlockSpec((tm, A_W), row),
        pl.BlockSpec((tm, B_W), row),
        pl.BlockSpec((tm, C_W), row),
    )
    in_specs = [
        pl.BlockSpec((tm, D_MODEL), row),
        pl.BlockSpec((1, D_MODEL), const),
        pl.BlockSpec((D_MODEL, nw), const, pipeline_mode=pl.Buffered(1)),
        pl.BlockSpec((3, tm, LANES), tab),
        pl.BlockSpec((3, tm, LANES), tab),
        pl.BlockSpec((1, LANES), const),
        pl.BlockSpec((1, LANES), const),
        pl.BlockSpec((B_W, B_W), const),
    ]
    return pl.pallas_call(
        _inproj_kernel,
        grid=(T // tm,),
        in_specs=in_specs,
        out_specs=out_specs,
        out_shape=out_shape,
        scratch_shapes=[pltpu.VMEM((QKV_W // LANES, tm, LANES), _F32)],
        compiler_params=_cparams(1),
        name="inproj",
    )(x2d, norm_pre, w, tab_a, tab_b, gq, gk, bd)


def _softmax_pv(q2, kp, vp, bias):
    s = _dot_nt(q2, kp) + bias
    m = jnp.max(s, axis=1, keepdims=True)
    e = jnp.exp2(s - m).astype(_BF16)
    return jnp.dot(e, jnp.concatenate([vp, jnp.ones_like(vp)], axis=1), preferred_element_type=_F32), m


def _unstack_heads(ol, m):
    n = ol.shape[0] // 2
    lo = _lane_lo((n, LANES))
    num = jnp.where(lo, ol[:n, :LANES], ol[n:, :LANES])
    den = jnp.where(lo, ol[:n, LANES:], ol[n:, LANES:])
    return num, den, jnp.where(lo, m[:n], m[n:])


def _finish_heads(ol, m):
    num, den, mx = _unstack_heads(ol, m)
    return num / den, mx + jnp.log2(den)


def _band_tile(q, k_ref, v_ref, kidx, kstart, nk, cols, bias):
    kp = k_ref[kidx + (pl.ds(kstart, nk), cols)]
    vp = v_ref[kidx + (pl.ds(kstart, nk), cols)]
    return _finish_heads(*_softmax_pv(_stack_heads(q), kp, vp, bias))


def _band_delta(sub, nk):
    row = lax.broadcasted_iota(jnp.int32, (2 * sub, nk), 0) % sub
    col = lax.broadcasted_iota(jnp.int32, (2 * sub, nk), 1)
    return col - row


def _band_bias(delta, shift, radius):
    rel = delta + shift
    return jnp.where((rel >= -radius) & (rel <= radius), 0.0, NEG_INF).astype(_F32)


def _attn_a_dil_kernel(q_ref, k_ref, v_ref, o_ref, lse_ref, *, dil, q_block, seq_len, radius, sub):
    nk = min(sub + 2 * radius, seq_len)
    base = pl.program_id(1) * q_block
    delta = _band_delta(sub, nk)
    windows = []
    for si in range(q_block // sub):
        l0 = base + si * sub
        kstart = pl.multiple_of(jnp.clip(l0 - radius, 0, seq_len - nk), HEAD_DIM)
        windows.append((kstart, _band_bias(delta, kstart - l0, radius)))

    def body(res, carry):
        for si, (kstart, bias) in enumerate(windows):
            for p in range(A_TILES):
                cols = slice(p * LANES, (p + 1) * LANES)
                q = q_ref[res, si * sub:(si + 1) * sub, cols]
                o, lse = _band_tile(q, k_ref, v_ref, (res,), kstart, nk, cols, bias)
                rows = pl.ds(si * sub * dil + res, sub, stride=dil)
                o_ref[p, rows, :] = o
                lse_ref[p, rows, :] = lse
        return carry

    def pair_body(rp, carry):
        for p in range(A_TILES):
            cols = slice(p * LANES, (p + 1) * LANES)
            res = (2 * rp, 2 * rp + 1)
            q2 = jnp.concatenate([_stack_heads(q_ref[r, :, cols]) for r in res], axis=0)
            kp = jnp.concatenate([k_ref[r, :, cols] for r in res], axis=0)
            vp = jnp.concatenate([v_ref[r, :, cols] for r in res], axis=0)
            ol, m = _softmax_pv(q2, kp, vp, pair_bias)
            for i, r in enumerate(res):
                blk = slice(i * 2 * sub, (i + 1) * 2 * sub)
                rows = pl.ds(r, sub, stride=dil)
                o_ref[p, rows, :], lse_ref[p, rows, :] = _finish_heads(ol[blk], m[blk])
        return carry

    if seq_len == q_block == sub and 2 * nk <= 2 * LANES and dil % 2 == 0:
        bias = windows[0][1]
        neg = jnp.full_like(bias, NEG_INF)
        pair_bias = jnp.concatenate([jnp.concatenate([bias, neg], axis=1),
                                     jnp.concatenate([neg, bias], axis=1)], axis=0)
        lax.fori_loop(0, dil // 2, pair_body, 0, unroll=min(dil // 2, 4))
    else:
        lax.fori_loop(0, dil, body, 0, unroll=max(1, min(dil, 8 * sub // q_block)))


def _attn_a_dil(a_res, batch, seq, dil, radius):
    L = seq // dil
    tl = min(max(A_STEP_TOKENS // dil, A_SUB), L)
    sub = min(A_SUB, tl)
    n_lt = L // tl
    T = batch * seq
    kern = functools.partial(_attn_a_dil_kernel, dil=dil, q_block=tl, seq_len=L, radius=radius, sub=sub)
    q_spec = pl.BlockSpec((dil, tl, A_W), lambda b, t: (0, b * n_lt + t, 0))
    k_spec = pl.BlockSpec((dil, L, A_W), lambda b, t: (0, b, 1))
    v_spec = pl.BlockSpec((dil, L, A_W), lambda b, t: (0, b, 2))
    o_spec = pl.BlockSpec((A_TILES, tl * dil, LANES), lambda b, t: (0, b * n_lt + t, 0))
    out_shape = (jax.ShapeDtypeStruct((A_TILES, T, LANES), _F32),) * 2
    return pl.pallas_call(
        kern,
        grid=(batch, n_lt),
        in_specs=[q_spec, k_spec, v_spec],
        out_specs=(o_spec, o_spec),
        out_shape=out_shape,
        compiler_params=_cparams(2),
        name=f"attn_a_d{dil}",
    )(a_res, a_res, a_res)


def _attn_a_final_kernel(q_ref, k_ref, v_ref, o4_ref, l4_ref, o16_ref, l16_ref, z_ref, g_ref, y_ref,
                         *, q_block, seq_len, radius, sub):
    nk = min(sub + 2 * radius, seq_len)
    n_sub = q_block // sub
    base = pl.program_id(1) * q_block
    delta = _band_delta(sub, nk)
    inner_bias = _band_bias(delta, -radius, radius)
    for si in range(n_sub):
        l0 = base + si * sub
        kstart = pl.multiple_of(jnp.clip(l0 - radius, 0, seq_len - nk), HEAD_DIM)
        inner = nk == sub + 2 * radius and 0 < si < n_sub - 1
        bias = inner_bias if inner else _band_bias(delta, kstart - l0, radius)
        rows = slice(si * sub, (si + 1) * sub)
        tiles = []
        for p in range(A_TILES):
            cols = slice(p * LANES, (p + 1) * LANES)
            kp = k_ref[pl.ds(kstart, nk), cols]
            vp = v_ref[pl.ds(kstart, nk), cols]
            num1, den1, m1 = _unstack_heads(*_softmax_pv(_stack_heads(q_ref[rows, cols]), kp, vp, bias))
            l4, l16 = l4_ref[p, rows, :], l16_ref[p, rows, :]
            m = jnp.maximum(jnp.maximum(m1, l4), l16)
            w1, e4, e16 = jnp.exp2(m1 - m), jnp.exp2(l4 - m), jnp.exp2(l16 - m)
            mix = w1 * num1 + e4 * o4_ref[p, rows, :] + e16 * o16_ref[p, rows, :]
            tiles.append(mix / (w1 * den1 + e4 + e16))
        ya = jnp.concatenate(tiles, axis=1)
        y_ref[rows, :] = _gated_norm(ya, z_ref[rows, :], g_ref[...]).astype(_BF16)


def _attn_a_final(a1, o4, l4, o16, l16, az, gain, batch, seq, radius):
    T = a1.shape[0]
    tl = min(A1_Q_BLOCK, seq)
    sub = min(A_SUB, tl)
    n_lt = seq // tl
    kern = functools.partial(_attn_a_final_kernel, q_block=tl, seq_len=seq, radius=radius, sub=sub)
    tok = lambda b, t: (b * n_lt + t, 0)
    tok3 = lambda b, t: (0, b * n_lt + t, 0)
    st_spec = pl.BlockSpec((A_TILES, tl, LANES), tok3)
    return pl.pallas_call(
        kern,
        grid=(batch, n_lt),
        in_specs=[
            pl.BlockSpec((tl, A_W), tok),
            pl.BlockSpec((seq, A_W), lambda b, t: (b, 1)),
            pl.BlockSpec((seq, A_W), lambda b, t: (b, 2)),
            st_spec, st_spec, st_spec, st_spec,
            pl.BlockSpec((tl, A_W), tok),
            pl.BlockSpec((1, A_W), lambda b, t: (0, 0)),
        ],
        out_specs=pl.BlockSpec((tl, A_W), tok),
        out_shape=jax.ShapeDtypeStruct((T, A_W), _BF16),
        compiler_params=_cparams(2),
        name="attn_a_d1",
    )(a1, a1, a1, o4, l4, o16, l16, az, gain)


def _attn_b_kernel(q_ref, kv_ref, z_ref, g_ref, y_ref, *, seq, tk, tq):
    n_grp = B_W // LANES
    n_kv = seq // tk
    for h in range(q_ref.shape[0] // tq):
        rows = slice(h * tq, (h + 1) * tq)
        q2s = [_stack_heads(q_ref[rows, g * LANES:(g + 1) * LANES]) for g in range(n_grp)]

        def body(kt, carry):
            ks = pl.multiple_of(kt * tk, tk)
            out = []
            for g in range(n_grp):
                m, acc = carry[g]
                k = kv_ref[pl.ds(ks, tk), g * LANES:(g + 1) * LANES]
                v = kv_ref[pl.ds(ks, tk), B_W + g * LANES:B_W + (g + 1) * LANES]
                s = _dot_nt(q2s[g], k)
                m_new = jnp.maximum(m, jnp.max(s, axis=1, keepdims=True))
                alpha = jnp.exp2(m - m_new)
                p = jnp.exp2(s - m_new)
                acc = alpha * acc + jnp.dot(p.astype(_BF16), v, preferred_element_type=_F32)
                out.append((m_new, acc))
            return tuple(out)

        init = tuple((jnp.full((2 * tq, 1), NEG_INF, _F32), jnp.zeros((2 * tq, LANES), _F32))
                     for _ in range(n_grp))
        final = (body(0, init) if n_kv == 1
                 else lax.fori_loop(0, n_kv, body, init, unroll=min(n_kv, B_KV_UNROLL)))
        tiles = []
        for g in range(n_grp):
            acc = final[g][1]
            head0, head1 = acc[:tq], acc[tq:]
            tiles.append(jnp.where(_lane_lo((tq, LANES)), head0 / pltpu.roll(head0, HEAD_DIM, axis=1),
                                   pltpu.roll(head1, HEAD_DIM, axis=1) / head1))
        yb = jnp.concatenate(tiles, axis=1)
        y_ref[rows, :] = _gated_norm(yb, z_ref[rows, :], g_ref[...]).astype(_BF16)


def _attn_b(bq, bkv, bz, gain, batch, seq):
    T = bq.shape[0]
    tq = min(B_TQ, seq)
    tk = min(B_TK, seq)
    step_rows = tq * (2 if seq == tk and seq % (2 * tq) == 0 else 1)
    nq = seq // step_rows
    kern = functools.partial(_attn_b_kernel, seq=seq, tk=tk, tq=tq)
    tok = lambda b, t: (b * nq + t, 0)
    return pl.pallas_call(
        kern,
        grid=(batch, nq),
        in_specs=[
            pl.BlockSpec((step_rows, B_W), tok),
            pl.BlockSpec((seq, 2 * B_W), lambda b, t: (b, 0)),
            pl.BlockSpec((step_rows, B_W), tok),
            pl.BlockSpec((1, B_W), lambda b, t: (0, 0)),
        ],
        out_specs=pl.BlockSpec((step_rows, B_W), tok),
        out_shape=jax.ShapeDtypeStruct((T, B_W), _BF16),
        compiler_params=_cparams(2),
        name="attn_b",
    )(bq, bkv, bz, gain)


def _c_tile_types():
    half = NA_ROWS // 2
    first = dict(ro_base=NA_ROWS - 1, start=[0] * C_ROWS)
    inner = dict(ro_base=NA_ROWS - 1 - half, start=list(range(C_ROWS)))
    last = dict(ro_base=NA_ROWS - 1 - (C_WIN - C_ROWS), start=[C_WIN - NA_ROWS] * C_ROWS)
    return (first, inner, last)


def _bias_kernel(rb_ref, out_ref):
    h = pl.program_id(0)
    n_ro = 2 * NA_ROWS - 1
    n_co = 2 * NA_COLS - 1
    qi = lax.broadcasted_iota(jnp.int32, (GRID_W, LANES), 0)
    lane = lax.broadcasted_iota(jnp.int32, (GRID_W, LANES), 1)
    kc = lane % GRID_W
    d = kc - qi
    cs = jnp.clip(qi - NA_COLS // 2, 0, GRID_W - NA_COLS)
    col_valid = (kc >= cs) & (kc < cs + NA_COLS)
    hits = [d == co - (NA_COLS - 1) for co in range(n_co)]
    neg = jnp.full((GRID_W, LANES), NEG_INF, _F32)
    rows = []
    for ro in range(n_ro):
        e = neg
        for co in range(n_co):
            e = jnp.where(hits[co], rb_ref[(h * n_ro + ro) * n_co + co] * LOG2E, e)
        rows.append(jnp.where(col_valid, e, neg))
    lo = lane < GRID_W
    for t, spec in enumerate(_c_tile_types()):
        for j in range(C_ROWS):
            def blk(kk):
                ok = spec["start"][j] <= kk < spec["start"][j] + NA_ROWS
                return rows[kk - j + spec["ro_base"]] if ok else neg
            tiles = [jnp.where(lo, blk(2 * i), blk(2 * i + 1)) for i in range(C_WIN // 2)]
            out_ref[t, j * GRID_W:(j + 1) * GRID_W, :] = jnp.concatenate(tiles, axis=1)


def _bias_table(rel_bias):
    flat = rel_bias.reshape(-1)
    return pl.pallas_call(
        _bias_kernel,
        grid=(C_HEADS,),
        in_specs=[pl.BlockSpec(memory_space=pltpu.SMEM)],
        out_specs=pl.BlockSpec((3, None, C_TQ, C_NK), lambda h: (0, h, 0, 0)),
        out_shape=jax.ShapeDtypeStruct((3, C_HEADS, C_TQ, C_NK), _F32),
        compiler_params=_cparams(1),
        name="c_bias",
    )(flat)


def _attn_c_kernel(q_ref, k_ref, v_ref, b_ref, z_ref, g_ref, y_ref, *, grid_rows):
    n_tiles = grid_rows // C_ROWS
    for h in range(C_STEP_TILES):
        tile = pl.program_id(1) * C_STEP_TILES + h
        kind = jnp.where(tile == 0, 0, jnp.where(tile == n_tiles - 1, 2, 1))
        r0 = tile * C_ROWS
        ws = jnp.clip(r0 - NA_ROWS // 2, 0, grid_rows - C_WIN)
        ks = pl.multiple_of(ws * GRID_W, GRID_W)
        rows = slice(h * C_TQ, (h + 1) * C_TQ)
        tiles = []
        for p in range(C_W // LANES):
            cols = slice(p * LANES, (p + 1) * LANES)
            num, den, _ = _unstack_heads(*_softmax_pv(_stack_heads(q_ref[rows, cols]), k_ref[pl.ds(ks, C_NK), cols],
                                                      v_ref[pl.ds(ks, C_NK), cols], b_ref[kind, p]))
            tiles.append(num / den)
        yc = jnp.concatenate(tiles, axis=1)
        y_ref[rows, :] = _gated_norm(yc, z_ref[rows, :], g_ref[...]).astype(_BF16)


def _attn_c(cqkv, bias, cz, gain, batch, seq):
    T = cqkv.shape[0]
    grid_rows = seq // GRID_W
    n_tiles = seq // C_TQ
    assert grid_rows >= C_WIN and n_tiles % C_STEP_TILES == 0
    nq = n_tiles // C_STEP_TILES
    tq = C_STEP_TILES * C_TQ
    n_pair = C_W // LANES
    bias = bias.reshape(3, n_pair, 2 * C_TQ, C_NK)

    bias_spec = pl.BlockSpec((3, n_pair, 2 * C_TQ, C_NK), lambda b, t: (0, 0, 0, 0), pipeline_mode=pl.Buffered(1))
    kern = functools.partial(_attn_c_kernel, grid_rows=grid_rows)
    tok = lambda b, t: (b * nq + t, 0)
    kv_buffers = 2 if seq * C_W * 2 <= C_KV_PREFETCH_BYTES else 1
    resident = lambda col: pl.BlockSpec((seq, C_W), lambda b, t: (b, col), pipeline_mode=pl.Buffered(kv_buffers))
    return pl.pallas_call(
        kern,
        grid=(batch, nq),
        in_specs=[pl.BlockSpec((tq, C_W), tok), resident(1), resident(2), bias_spec]
        + [pl.BlockSpec((tq, C_W), tok), pl.BlockSpec((1, C_W), lambda b, t: (0, 0))],
        out_specs=pl.BlockSpec((tq, C_W), tok),
        out_shape=jax.ShapeDtypeStruct((T, C_W), _BF16),
        compiler_params=_cparams(2),
        name="attn_c",
    )(cqkv, cqkv, cqkv, bias, cz, gain)


def _outproj_kernel(ya_ref, yb_ref, yc_ref, w_ref, npost_ref, x_ref, out_ref):
    y = jnp.concatenate([ya_ref[...], yb_ref[...], yc_ref[...]], axis=1)
    t = jnp.dot(y, w_ref[...], preferred_element_type=_F32)
    r = lax.rsqrt(jnp.mean(t * t, axis=-1, keepdims=True) + EPS)
    out_ref[...] = x_ref[...] + t * r * npost_ref[...]


def _outproj(ya, yb, yc, w, npost, x2d):
    T = x2d.shape[0]
    tm = min(OUT_ROW_TILE, T)
    row = lambda i: (i, 0)
    const = lambda i: (0, 0)
    return pl.pallas_call(
        _outproj_kernel,
        grid=(T // tm,),
        in_specs=[pl.BlockSpec((tm, A_W), row), pl.BlockSpec((tm, B_W), row), pl.BlockSpec((tm, C_W), row),
                  pl.BlockSpec((MIX_W, D_MODEL), const, pipeline_mode=pl.Buffered(1)),
                  pl.BlockSpec((1, D_MODEL), const), pl.BlockSpec((tm, D_MODEL), row)],
        out_specs=pl.BlockSpec((tm, D_MODEL), row),
        out_shape=jax.ShapeDtypeStruct((T, D_MODEL), _F32),
        compiler_params=_cparams(1),
        name="outproj",
    )(ya, yb, yc, w, npost, x2d)


def _lane_freqs(freqs, n_rot, width):
    lane = np.arange(LANES) % width
    idx = lane % (n_rot // 2)
    rot = lane < n_rot
    first = rot & (lane % n_rot < n_rot // 2)
    second = rot & ~first
    return jnp.where(rot, freqs[idx], 0.0), first, second


def _rotary_tables(ang, first, second):
    cos, sin = jnp.cos(ang), jnp.sin(ang)
    zero = jnp.zeros_like(sin)
    return jnp.stack([cos, jnp.where(first, -sin, zero), jnp.where(second, sin, zero)])


def _rope_tables(seq):
    freqs = ROPE_THETA ** (-jnp.arange(0, ROPE_DIMS, 2, dtype=_F32) / ROPE_DIMS)
    lane_f, first, second = _lane_freqs(freqs, ROPE_DIMS, HEAD_DIM)
    ang = jnp.arange(seq, dtype=_F32)[:, None] * lane_f[None, :]
    return _rotary_tables(ang, first, second)


def _axial_tables(seq):
    half = HEAD_DIM // 2
    freqs = AXIAL_THETA ** (-jnp.arange(0, half, 2, dtype=_F32) / half)
    lane_f, first, second = _lane_freqs(freqs, half, half)
    t = jnp.arange(seq)
    by_row = (np.arange(LANES) % HEAD_DIM) < half
    pos = jnp.where(by_row[None, :], (t // GRID_W).astype(_F32)[:, None], (t % GRID_W).astype(_F32)[:, None])
    return _rotary_tables(pos * lane_f[None, :], first, second)


def _layer(x2d, batch, seq, tabs, norm_pre, w_in, gq, gk, bd, bias, gains, w_out, npost):
    tab_a, tab_b = tabs
    ga, gb, gc = gains
    a1, a4, a16, cqkv, bq, bkv, az, bz, cz = _inproj(x2d, seq, norm_pre, w_in, tab_a, tab_b, gq, gk, bd)
    radii = {dil: window // (2 * dil) for window, dil in A_PATTERNS}
    o4, l4 = _attn_a_dil(a4, batch, seq, 4, radii[4])
    o16, l16 = _attn_a_dil(a16, batch, seq, 16, radii[16])
    ya = _attn_a_final(a1, o4, l4, o16, l16, az, ga, batch, seq, radii[1])
    yb = _attn_b(bq, bkv, bz, gb, batch, seq)
    yc = _attn_c(cqkv, bias, cz, gc, batch, seq)
    return _outproj(ya, yb, yc, w_out, npost, x2d)


def kernel(x_prompt, x_sample, norm_pre, w_in, q_norm, k_norm, rel_bias, branch_gain, w_out, norm_post):
    depth = w_in.shape[0]
    w_in_p = w_in.astype(_BF16)
    w_out_p = w_out.astype(_BF16)
    bd = jnp.asarray(np.kron(np.eye(B_W // HEAD_DIM), np.full((HEAD_DIM, HEAD_DIM), 1.0 / HEAD_DIM)), _BF16)
    max_seq = max(x_prompt.shape[1], x_sample.shape[1])
    tabs = (_rope_tables(max_seq), _axial_tables(max_seq))
    groups = []
    for x in (x_prompt, x_sample):
        batch, seq, _ = x.shape
        groups.append(dict(y=x.reshape(batch * seq, D_MODEL), batch=batch, seq=seq, tabs=tabs))
    for l in range(depth):
        bias = _bias_table(rel_bias[l])
        gq = jnp.tile(q_norm[l][None, :], (1, LANES // HEAD_DIM))
        gk = jnp.tile(k_norm[l][None, :], (1, LANES // HEAD_DIM))
        gain = branch_gain[l][None, :]
        gains = (gain[:, 0:A_W], gain[:, A_W:A_W + B_W], gain[:, A_W + B_W:MIX_W])
        for g in groups:
            g["y"] = _layer(g["y"], g["batch"], g["seq"], g["tabs"], norm_pre[l][None, :], w_in_p[l],
                            gq, gk, bd, bias, gains, w_out_p[l], norm_post[l][None, :])
    return tuple(g["y"].reshape(g["batch"], g["seq"], D_MODEL) for g in groups)
```

```python
import functools
import math

import jax
import jax.numpy as jnp
import numpy as np
from jax import lax
from jax.experimental import pallas as pl
from jax.experimental.pallas import tpu as pltpu

D_MODEL = 1024
HEAD_DIM = 64
GRID_W = 64
EPS = 1e-6
A_HEADS = 6
A_PATTERNS = ((128, 1), (512, 4), (2048, 16))
ROPE_THETA = 500000.0
ROPE_DIMS = HEAD_DIM // 4
B_HEADS = 4
B_KV_HEADS = 2
AXIAL_THETA = 10000.0
C_HEADS = 6
NA_ROWS = 8
NA_COLS = 16

A_W = A_HEADS * HEAD_DIM
B_W = B_HEADS * HEAD_DIM
B_KV_W = B_KV_HEADS * HEAD_DIM
C_W = C_HEADS * HEAD_DIM
MIX_W = A_W + B_W + C_W
QKV_W = 3 * A_W
OFF_A = 0
OFF_AZ = OFF_A + QKV_W
OFF_B = OFF_AZ + A_W
OFF_BZ = OFF_B + B_W + 2 * B_KV_W
OFF_C = OFF_BZ + B_W
OFF_CZ = OFF_C + QKV_W

LANES = 128
A_TILES = A_W // LANES
Q_SCALE = HEAD_DIM ** -0.5
LOG2E = math.log2(math.e)
VMEM_LIMIT = 56 * 1024 * 1024

IN_ROW_TILE = 512
OUT_ROW_TILE = 1024
A_SUB = 128
A_STEP_TOKENS = 2048
A1_Q_BLOCK = 1024
B_TQ = 512
B_TK = 2048
B_KV_UNROLL = 4
C_ROWS = 4
C_WIN = 12
C_TQ = C_ROWS * GRID_W
C_STEP_TILES = 8
C_NK = C_WIN * GRID_W
C_KV_PREFETCH_BYTES = 2 * 1024 * 1024
NEG_INF = float("-inf")

_F32 = jnp.float32
_BF16 = jnp.bfloat16


def _cparams(n_grid):
    return pltpu.CompilerParams(dimension_semantics=("arbitrary",) * n_grid,
                                vmem_limit_bytes=VMEM_LIMIT)


def _lane_lo(shape):
    return lax.broadcasted_iota(jnp.int32, shape, len(shape) - 1) % LANES < HEAD_DIM


def _stack_heads(q):
    lo = _lane_lo(q.shape)
    zero = jnp.zeros_like(q)
    return jnp.concatenate([jnp.where(lo, q, zero), jnp.where(lo, zero, q)], axis=0)


def _dot_nt(a, b):
    return lax.dot_general(a, b, (((1,), (1,)), ((), ())), preferred_element_type=_F32)


def _gated_norm(y, z, gain):
    hz = 0.5 * z.astype(_F32)
    u = y * (hz + hz * jnp.tanh(hz))
    r = lax.rsqrt(jnp.mean(u * u, axis=-1, keepdims=True) + EPS)
    return u * r * gain


def _rotate_tile(t, tab_ref, shift):
    up = pltpu.roll(t, LANES - shift, axis=1)
    dn = pltpu.roll(t, shift, axis=1)
    return t * tab_ref[0] + up * tab_ref[1] + dn * tab_ref[2]


def _inproj_kernel(x_ref, g_ref, w_ref, ta_ref, tb_ref, gq_ref, gk_ref, bd_ref,
                   a1_ref, a4_ref, a16_ref, c_ref, bq_ref, bkv_ref, az_ref, bz_ref, cz_ref, scr_ref):
    x = x_ref[...]
    tm = x.shape[0]
    r = lax.rsqrt(jnp.mean(x * x, axis=-1, keepdims=True) + EPS)
    h = (x * r * g_ref[...]).astype(_BF16)
    p = jnp.dot(h, w_ref[...], preferred_element_type=_F32)

    for j in range(QKV_W // LANES):
        t = p[:, OFF_A + j * LANES:OFF_A + (j + 1) * LANES]
        if j < 2 * A_TILES:
            t = _rotate_tile(t, ta_ref, ROPE_DIMS // 2)
        if j < A_TILES:
            t = t * (Q_SCALE * LOG2E)
        a1_ref[:, j * LANES:(j + 1) * LANES] = t.astype(_BF16)
        scr_ref[j] = t
    for dil, ref in ((4, a4_ref), (16, a16_ref)):
        for res in range(dil):
            for j in range(QKV_W // LANES):
                rows = scr_ref[j, pl.ds(res, tm // dil, stride=dil), :]
                ref[res, :, j * LANES:(j + 1) * LANES] = rows.astype(_BF16)

    c_ref[:, 0:C_W] = (p[:, OFF_C:OFF_C + C_W] * (Q_SCALE * LOG2E)).astype(_BF16)
    c_ref[:, C_W:QKV_W] = p[:, OFF_C + C_W:OFF_C + QKV_W].astype(_BF16)

    bd = bd_ref[...]

    def headnorm(t, gain):
        width = t.shape[1]
        sq = t * t
        hi = sq.astype(_BF16)
        lo = (sq - hi.astype(_F32)).astype(_BF16)
        ms = (jnp.dot(hi, bd[:width, :width], preferred_element_type=_F32)
              + jnp.dot(lo, bd[:width, :width], preferred_element_type=_F32))
        return t * lax.rsqrt(ms + EPS) * gain

    qn = headnorm(p[:, OFF_B:OFF_B + B_W], jnp.concatenate([gq_ref[...]] * (B_W // LANES), axis=1))
    for j in range(B_W // LANES):
        t = _rotate_tile(qn[:, j * LANES:(j + 1) * LANES], tb_ref, HEAD_DIM // 4) * (Q_SCALE * LOG2E)
        bq_ref[:, j * LANES:(j + 1) * LANES] = t.astype(_BF16)
    lo_half = _lane_lo((tm, LANES))
    kn = _rotate_tile(headnorm(p[:, OFF_B + B_W:OFF_B + B_W + B_KV_W], gk_ref[...]), tb_ref, HEAD_DIM // 4)
    k_sw = pltpu.roll(kn, HEAD_DIM, axis=1)
    bkv_ref[:, 0:LANES] = jnp.where(lo_half, kn, k_sw).astype(_BF16)
    bkv_ref[:, LANES:2 * LANES] = jnp.where(lo_half, k_sw, kn).astype(_BF16)
    vv = p[:, OFF_B + B_W + B_KV_W:OFF_B + B_W + 2 * B_KV_W]
    bkv_ref[:, B_W:B_W + LANES] = jnp.where(lo_half, vv, 1.0).astype(_BF16)
    bkv_ref[:, B_W + LANES:2 * B_W] = jnp.where(lo_half, pltpu.roll(vv, HEAD_DIM, axis=1), 1.0).astype(_BF16)

    az_ref[...] = p[:, OFF_AZ:OFF_AZ + A_W].astype(_BF16)
    bz_ref[...] = p[:, OFF_BZ:OFF_BZ + B_W].astype(_BF16)
    cz_ref[...] = p[:, OFF_CZ:OFF_CZ + C_W].astype(_BF16)


def _inproj(x2d, seq, norm_pre, w, tab_a, tab_b, gq, gk, bd):
    T = x2d.shape[0]
    tm = min(IN_ROW_TILE, seq)
    n_seq = seq // tm
    nw = w.shape[1]
    const = lambda i: (0, 0)
    row = lambda i: (i, 0)
    row3 = lambda i: (0, i, 0)
    tab = lambda i: (0, i % n_seq, 0)
    out_shape = (
        jax.ShapeDtypeStruct((T, QKV_W), _BF16),
        jax.ShapeDtypeStruct((4, T // 4, QKV_W), _BF16),
        jax.ShapeDtypeStruct((16, T // 16, QKV_W), _BF16),
        jax.ShapeDtypeStruct((T, QKV_W), _BF16),
        jax.ShapeDtypeStruct((T, B_W), _BF16),
        jax.ShapeDtypeStruct((T, 2 * B_W), _BF16),
        jax.ShapeDtypeStruct((T, A_W), _BF16),
        jax.ShapeDtypeStruct((T, B_W), _BF16),
        jax.ShapeDtypeStruct((T, C_W), _BF16),
    )
    out_specs = (
        pl.BlockSpec((tm, QKV_W), row),
        pl.BlockSpec((4, tm // 4, QKV_W), row3),
        pl.BlockSpec((16, tm // 16, QKV_W), row3),
        pl.BlockSpec((tm, QKV_W), row),
        pl.BlockSpec((tm, B_W), row),
        pl.BlockSpec((tm, 2 * B_W), row),
        pl.BlockSpec((tm, A_W), row),
        pl.BlockSpec((tm, B_W), row),
        pl.BlockSpec((tm, C_W), row),
    )
    in_specs = [
        pl.BlockSpec((tm, D_MODEL), row),
        pl.BlockSpec((1, D_MODEL), const),
        pl.BlockSpec((D_MODEL, nw), const, pipeline_mode=pl.Buffered(1)),
        pl.BlockSpec((3, tm, LANES), tab),
        pl.BlockSpec((3, tm, LANES), tab),
        pl.BlockSpec((1, LANES), const),
        pl.BlockSpec((1, LANES), const),
        pl.BlockSpec((B_W, B_W), const),
    ]
    return pl.pallas_call(
        _inproj_kernel,
        grid=(T // tm,),
        in_specs=in_specs,
        out_specs=out_specs,
        out_shape=out_shape,
        scratch_shapes=[pltpu.VMEM((QKV_W // LANES, tm, LANES), _F32)],
        compiler_params=_cparams(1),
        name="inproj",
    )(x2d, norm_pre, w, tab_a, tab_b, gq, gk, bd)


def _softmax_pv(q2, kp, vp, bias):
    s = _dot_nt(q2, kp) + bias
    m = jnp.max(s, axis=1, keepdims=True)
    e = jnp.exp2(s - m).astype(_BF16)
    return jnp.dot(e, jnp.concatenate([vp, jnp.ones_like(vp)], axis=1), preferred_element_type=_F32), m


def _unstack_heads(ol, m):
    n = ol.shape[0] // 2
    lo = _lane_lo((n, LANES))
    num = jnp.where(lo, ol[:n, :LANES], ol[n:, :LANES])
    den = jnp.where(lo, ol[:n, LANES:], ol[n:, LANES:])
    return num, den, jnp.where(lo, m[:n], m[n:])


def _finish_heads(ol, m):
    num, den, mx = _unstack_heads(ol, m)
    return num / den, mx + jnp.log2(den)


def _band_tile(q, k_ref, v_ref, kidx, kstart, nk, cols, bias):
    kp = k_ref[kidx + (pl.ds(kstart, nk), cols)]
    vp = v_ref[kidx + (pl.ds(kstart, nk), cols)]
    return _finish_heads(*_softmax_pv(_stack_heads(q), kp, vp, bias))


def _band_delta(sub, nk):
    row = lax.broadcasted_iota(jnp.int32, (2 * sub, nk), 0) % sub
    col = lax.broadcasted_iota(jnp.int32, (2 * sub, nk), 1)
    return col - row


def _band_bias(delta, shift, radius):
    rel = delta + shift
    return jnp.where((rel >= -radius) & (rel <= radius), 0.0, NEG_INF).astype(_F32)


def _attn_a_dil_kernel(q_ref, k_ref, v_ref, o_ref, lse_ref, *, dil, q_block, seq_len, radius, sub):
    nk = min(sub + 2 * radius, seq_len)
    base = pl.program_id(1) * q_block
    delta = _band_delta(sub, nk)
    windows = []
    for si in range(q_block // sub):
        l0 = base + si * sub
        kstart = pl.multiple_of(jnp.clip(l0 - radius, 0, seq_len - nk), HEAD_DIM)
        windows.append((kstart, _band_bias(delta, kstart - l0, radius)))

    def body(res, carry):
        for si, (kstart, bias) in enumerate(windows):
            for p in range(A_TILES):
                cols = slice(p * LANES, (p + 1) * LANES)
                q = q_ref[res, si * sub:(si + 1) * sub, cols]
                o, lse = _band_tile(q, k_ref, v_ref, (res,), kstart, nk, cols, bias)
                rows = pl.ds(si * sub * dil + res, sub, stride=dil)
                o_ref[p, rows, :] = o
                lse_ref[p, rows, :] = lse
        return carry

    def pair_body(rp, carry):
        for p in range(A_TILES):
            cols = slice(p * LANES, (p + 1) * LANES)
            res = (2 * rp, 2 * rp + 1)
            q2 = jnp.concatenate([_stack_heads(q_ref[r, :, cols]) for r in res], axis=0)
            kp = jnp.concatenate([k_ref[r, :, cols] for r in res], axis=0)
            vp = jnp.concatenate([v_ref[r, :, cols] for r in res], axis=0)
            ol, m = _softmax_pv(q2, kp, vp, pair_bias)
            for i, r in enumerate(res):
                blk = slice(i * 2 * sub, (i + 1) * 2 * sub)
                rows = pl.ds(r, sub, stride=dil)
                o_ref[p, rows, :], lse_ref[p, rows, :] = _finish_heads(ol[blk], m[blk])
        return carry

    if seq_len == q_block == sub and 2 * nk <= 2 * LANES and dil % 2 == 0:
        bias = windows[0][1]
        neg = jnp.full_like(bias, NEG_INF)
        pair_bias = jnp.concatenate([jnp.concatenate([bias, neg], axis=1),
                                     jnp.concatenate([neg, bias], axis=1)], axis=0)
        lax.fori_loop(0, dil // 2, pair_body, 0, unroll=min(dil // 2, 4))
    else:
        lax.fori_loop(0, dil, body, 0, unroll=max(1, min(dil, 8 * sub // q_block)))


def _attn_a_dil(a_res, batch, seq, dil, radius):
    L = seq // dil
    tl = min(max(A_STEP_TOKENS // dil, A_SUB), L)
    sub = min(A_SUB, tl)
    n_lt = L // tl
    T = batch * seq
    kern = functools.partial(_attn_a_dil_kernel, dil=dil, q_block=tl, seq_len=L, radius=radius, sub=sub)
    q_spec = pl.BlockSpec((dil, tl, A_W), lambda b, t: (0, b * n_lt + t, 0))
    k_spec = pl.BlockSpec((dil, L, A_W), lambda b, t: (0, b, 1))
    v_spec = pl.BlockSpec((dil, L, A_W), lambda b, t: (0, b, 2))
    o_spec = pl.BlockSpec((A_TILES, tl * dil, LANES), lambda b, t: (0, b * n_lt + t, 0))
    out_shape = (jax.ShapeDtypeStruct((A_TILES, T, LANES), _F32),) * 2
    return pl.pallas_call(
        kern,
        grid=(batch, n_lt),
        in_specs=[q_spec, k_spec, v_spec],
        out_specs=(o_spec, o_spec),
        out_shape=out_shape,
        compiler_params=_cparams(2),
        name=f"attn_a_d{dil}",
    )(a_res, a_res, a_res)


def _attn_a_final_kernel(q_ref, k_ref, v_ref, o4_ref, l4_ref, o16_ref, l16_ref, z_ref, g_ref, y_ref,
                         *, q_block, seq_len, radius, sub):
    nk = min(sub + 2 * radius, seq_len)
    n_sub = q_block // sub
    base = pl.program_id(1) * q_block
    delta = _band_delta(sub, nk)
    inner_bias = _band_bias(delta, -radius, radius)
    for si in range(n_sub):
        l0 = base + si * sub
        kstart = pl.multiple_of(jnp.clip(l0 - radius, 0, seq_len - nk), HEAD_DIM)
        inner = nk == sub + 2 * radius and 0 < si < n_sub - 1
        bias = inner_bias if inner else _band_bias(delta, kstart - l0, radius)
        rows = slice(si * sub, (si + 1) * sub)
        tiles = []
        for p in range(A_TILES):
            cols = slice(p * LANES, (p + 1) * LANES)
            kp = k_ref[pl.ds(kstart, nk), cols]
            vp = v_ref[pl.ds(kstart, nk), cols]
            num1, den1, m1 = _unstack_heads(*_softmax_pv(_stack_heads(q_ref[rows, cols]), kp, vp, bias))
            l4, l16 = l4_ref[p, rows, :], l16_ref[p, rows, :]
            m = jnp.maximum(jnp.maximum(m1, l4), l16)
            w1, e4, e16 = jnp.exp2(m1 - m), jnp.exp2(l4 - m), jnp.exp2(l16 - m)
            mix = w1 * num1 + e4 * o4_ref[p, rows, :] + e16 * o16_ref[p, rows, :]
            tiles.append(mix / (w1 * den1 + e4 + e16))
        ya = jnp.concatenate(tiles, axis=1)
        y_ref[rows, :] = _gated_norm(ya, z_ref[rows, :], g_ref[...]).astype(_BF16)


def _attn_a_final(a1, o4, l4, o16, l16, az, gain, batch, seq, radius):
    T = a1.shape[0]
    tl = min(A1_Q_BLOCK, seq)
    sub = min(A_SUB, tl)
    n_lt = seq // tl
    kern = functools.partial(_attn_a_final_kernel, q_block=tl, seq_len=seq, radius=radius, sub=sub)
    tok = lambda b, t: (b * n_lt + t, 0)
    tok3 = lambda b, t: (0, b * n_lt + t, 0)
    st_spec = pl.BlockSpec((A_TILES, tl, LANES), tok3)
    return pl.pallas_call(
        kern,
        grid=(batch, n_lt),
        in_specs=[
            pl.BlockSpec((tl, A_W), tok),
            pl.BlockSpec((seq, A_W), lambda b, t: (b, 1)),
            pl.BlockSpec((seq, A_W), lambda b, t: (b, 2)),
            st_spec, st_spec, st_spec, st_spec,
            pl.BlockSpec((tl, A_W), tok),
            pl.BlockSpec((1, A_W), lambda b, t: (0, 0)),
        ],
        out_specs=pl.BlockSpec((tl, A_W), tok),
        out_shape=jax.ShapeDtypeStruct((T, A_W), _BF16),
        compiler_params=_cparams(2),
        name="attn_a_d1",
    )(a1, a1, a1, o4, l4, o16, l16, az, gain)


def _attn_b_kernel(q_ref, kv_ref, z_ref, g_ref, y_ref, *, seq, tk, tq):
    n_grp = B_W // LANES
    n_kv = seq // tk
    for h in range(q_ref.shape[0] // tq):
        rows = slice(h * tq, (h + 1) * tq)
        q2s = [_stack_heads(q_ref[rows, g * LANES:(g + 1) * LANES]) for g in range(n_grp)]

        def body(kt, carry):
            ks = pl.multiple_of(kt * tk, tk)
            out = []
            for g in range(n_grp):
                m, acc = carry[g]
                k = kv_ref[pl.ds(ks, tk), g * LANES:(g + 1) * LANES]
                v = kv_ref[pl.ds(ks, tk), B_W + g * LANES:B_W + (g + 1) * LANES]
                s = _dot_nt(q2s[g], k)
                m_new = jnp.maximum(m, jnp.max(s, axis=1, keepdims=True))
                alpha = jnp.exp2(m - m_new)
                p = jnp.exp2(s - m_new)
                acc = alpha * acc + jnp.dot(p.astype(_BF16), v, preferred_element_type=_F32)
                out.append((m_new, acc))
            return tuple(out)

        init = tuple((jnp.full((2 * tq, 1), NEG_INF, _F32), jnp.zeros((2 * tq, LANES), _F32))
                     for _ in range(n_grp))
        final = (body(0, init) if n_kv == 1
                 else lax.fori_loop(0, n_kv, body, init, unroll=min(n_kv, B_KV_UNROLL)))
        tiles = []
        for g in range(n_grp):
            acc = final[g][1]
            head0, head1 = acc[:tq], acc[tq:]
            tiles.append(jnp.where(_lane_lo((tq, LANES)), head0 / pltpu.roll(head0, HEAD_DIM, axis=1),
                                   pltpu.roll(head1, HEAD_DIM, axis=1) / head1))
        yb = jnp.concatenate(tiles, axis=1)
        y_ref[rows, :] = _gated_norm(yb, z_ref[rows, :], g_ref[...]).astype(_BF16)


def _attn_b(bq, bkv, bz, gain, batch, seq):
    T = bq.shape[0]
    tq = min(B_TQ, seq)
    tk = min(B_TK, seq)
    step_rows = tq * (B_KV_UNROLL if seq == tk and seq % (B_KV_UNROLL * tq) == 0 else 1)
    nq = seq // step_rows
    kern = functools.partial(_attn_b_kernel, seq=seq, tk=tk, tq=tq)
    tok = lambda b, t: (b * nq + t, 0)
    return pl.pallas_call(
        kern,
        grid=(batch, nq),
        in_specs=[
            pl.BlockSpec((step_rows, B_W), tok),
            pl.BlockSpec((seq, 2 * B_W), lambda b, t: (b, 0)),
            pl.BlockSpec((step_rows, B_W), tok),
            pl.BlockSpec((1, B_W), lambda b, t: (0, 0)),
        ],
        out_specs=pl.BlockSpec((step_rows, B_W), tok),
        out_shape=jax.ShapeDtypeStruct((T, B_W), _BF16),
        compiler_params=_cparams(2),
        name="attn_b",
    )(bq, bkv, bz, gain)


def _c_tile_types():
    half = NA_ROWS // 2
    first = dict(ro_base=NA_ROWS - 1, start=[0] * C_ROWS)
    inner = dict(ro_base=NA_ROWS - 1 - half, start=list(range(C_ROWS)))
    last = dict(ro_base=NA_ROWS - 1 - (C_WIN - C_ROWS), start=[C_WIN - NA_ROWS] * C_ROWS)
    return (first, inner, last)


def _bias_kernel(rb_ref, out_ref):
    h = pl.program_id(0)
    n_ro = 2 * NA_ROWS - 1
    n_co = 2 * NA_COLS - 1
    qi = lax.broadcasted_iota(jnp.int32, (GRID_W, LANES), 0)
    lane = lax.broadcasted_iota(jnp.int32, (GRID_W, LANES), 1)
    kc = lane % GRID_W
    d = kc - qi
    cs = jnp.clip(qi - NA_COLS // 2, 0, GRID_W - NA_COLS)
    col_valid = (kc >= cs) & (kc < cs + NA_COLS)
    hits = [d == co - (NA_COLS - 1) for co in range(n_co)]
    neg = jnp.full((GRID_W, LANES), NEG_INF, _F32)
    rows = []
    for ro in range(n_ro):
        e = neg
        for co in range(n_co):
            e = jnp.where(hits[co], rb_ref[(h * n_ro + ro) * n_co + co] * LOG2E, e)
        rows.append(jnp.where(col_valid, e, neg))
    lo = lane < GRID_W
    for t, spec in enumerate(_c_tile_types()):
        for j in range(C_ROWS):
            def blk(kk):
                ok = spec["start"][j] <= kk < spec["start"][j] + NA_ROWS
                return rows[kk - j + spec["ro_base"]] if ok else neg
            tiles = [jnp.where(lo, blk(2 * i), blk(2 * i + 1)) for i in range(C_WIN // 2)]
            out_ref[t, j * GRID_W:(j + 1) * GRID_W, :] = jnp.concatenate(tiles, axis=1)


def _bias_table(rel_bias):
    flat = rel_bias.reshape(-1)
    return pl.pallas_call(
        _bias_kernel,
        grid=(C_HEADS,),
        in_specs=[pl.BlockSpec(memory_space=pltpu.SMEM)],
        out_specs=pl.BlockSpec((3, None, C_TQ, C_NK), lambda h: (0, h, 0, 0)),
        out_shape=jax.ShapeDtypeStruct((3, C_HEADS, C_TQ, C_NK), _F32),
        compiler_params=_cparams(1),
        name="c_bias",
    )(flat)


def _attn_c_kernel(q_ref, k_ref, v_ref, b_ref, z_ref, g_ref, y_ref, *, grid_rows):
    n_tiles = grid_rows // C_ROWS
    for h in range(C_STEP_TILES):
        tile = pl.program_id(1) * C_STEP_TILES + h
        kind = jnp.where(tile == 0, 0, jnp.where(tile == n_tiles - 1, 2, 1))
        r0 = tile * C_ROWS
        ws = jnp.clip(r0 - NA_ROWS // 2, 0, grid_rows - C_WIN)
        ks = pl.multiple_of(ws * GRID_W, GRID_W)
        rows = slice(h * C_TQ, (h + 1) * C_TQ)
        tiles = []
        for p in range(C_W // LANES):
            cols = slice(p * LANES, (p + 1) * LANES)
            num, den, _ = _unstack_heads(*_softmax_pv(_stack_heads(q_ref[rows, cols]), k_ref[pl.ds(ks, C_NK), cols],
                                                      v_ref[pl.ds(ks, C_NK), cols], b_ref[kind, p]))
            tiles.append(num / den)
        yc = jnp.concatenate(tiles, axis=1)
        y_ref[rows, :] = _gated_norm(yc, z_ref[rows, :], g_ref[...]).astype(_BF16)


def _attn_c(cqkv, bias, cz, gain, batch, seq):
    T = cqkv.shape[0]
    grid_rows = seq // GRID_W
    n_tiles = seq // C_TQ
    assert grid_rows >= C_WIN and n_tiles % C_STEP_TILES == 0
    nq = n_tiles // C_STEP_TILES
    tq = C_STEP_TILES * C_TQ
    n_pair = C_W // LANES
    bias = bias.reshape(3, n_pair, 2 * C_TQ, C_NK)

    bias_spec = pl.BlockSpec((3, n_pair, 2 * C_TQ, C_NK), lambda b, t: (0, 0, 0, 0), pipeline_mode=pl.Buffered(1))
    kern = functools.partial(_attn_c_kernel, grid_rows=grid_rows)
    tok = lambda b, t: (b * nq + t, 0)
    kv_buffers = 2 if seq * C_W * 2 <= C_KV_PREFETCH_BYTES else 1
    resident = lambda col: pl.BlockSpec((seq, C_W), lambda b, t: (b, col), pipeline_mode=pl.Buffered(kv_buffers))
    return pl.pallas_call(
        kern,
        grid=(batch, nq),
        in_specs=[pl.BlockSpec((tq, C_W), tok), resident(1), resident(2), bias_spec]
        + [pl.BlockSpec((tq, C_W), tok), pl.BlockSpec((1, C_W), lambda b, t: (0, 0))],
        out_specs=pl.BlockSpec((tq, C_W), tok),
        out_shape=jax.ShapeDtypeStruct((T, C_W), _BF16),
        compiler_params=_cparams(2),
        name="attn_c",
    )(cqkv, cqkv, cqkv, bias, cz, gain)


def _outproj_kernel(ya_ref, yb_ref, yc_ref, w_ref, npost_ref, x_ref, out_ref):
    y = jnp.concatenate([ya_ref[...], yb_ref[...], yc_ref[...]], axis=1)
    t = jnp.dot(y, w_ref[...], preferred_element_type=_F32)
    r = lax.rsqrt(jnp.mean(t * t, axis=-1, keepdims=True) + EPS)
    out_ref[...] = x_ref[...] + t * r * npost_ref[...]


def _outproj(ya, yb, yc, w, npost, x2d):
    T = x2d.shape[0]
    tm = min(OUT_ROW_TILE, T)
    row = lambda i: (i, 0)
    const = lambda i: (0, 0)
    return pl.pallas_call(
        _outproj_kernel,
        grid=(T // tm,),
        in_specs=[pl.BlockSpec((tm, A_W), row), pl.BlockSpec((tm, B_W), row), pl.BlockSpec((tm, C_W), row),
                  pl.BlockSpec((MIX_W, D_MODEL), const, pipeline_mode=pl.Buffered(1)),
                  pl.BlockSpec((1, D_MODEL), const), pl.BlockSpec((tm, D_MODEL), row)],
        out_specs=pl.BlockSpec((tm, D_MODEL), row),
        out_shape=jax.ShapeDtypeStruct((T, D_MODEL), _F32),
        compiler_params=_cparams(1),
        name="outproj",
    )(ya, yb, yc, w, npost, x2d)


def _lane_freqs(freqs, n_rot, width):
    lane = np.arange(LANES) % width
    idx = lane % (n_rot // 2)
    rot = lane < n_rot
    first = rot & (lane % n_rot < n_rot // 2)
    second = rot & ~first
    return jnp.where(rot, freqs[idx], 0.0), first, second


def _rotary_tables(ang, first, second):
    cos, sin = jnp.cos(ang), jnp.sin(ang)
    zero = jnp.zeros_like(sin)
    return jnp.stack([cos, jnp.where(first, -sin, zero), jnp.where(second, sin, zero)])


def _rope_tables(seq):
    freqs = ROPE_THETA ** (-jnp.arange(0, ROPE_DIMS, 2, dtype=_F32) / ROPE_DIMS)
    lane_f, first, second = _lane_freqs(freqs, ROPE_DIMS, HEAD_DIM)
    ang = jnp.arange(seq, dtype=_F32)[:, None] * lane_f[None, :]
    return _rotary_tables(ang, first, second)


def _axial_tables(seq):
    half = HEAD_DIM // 2
    freqs = AXIAL_THETA ** (-jnp.arange(0, half, 2, dtype=_F32) / half)
    lane_f, first, second = _lane_freqs(freqs, half, half)
    t = jnp.arange(seq)
    by_row = (np.arange(LANES) % HEAD_DIM) < half
    pos = jnp.where(by_row[None, :], (t // GRID_W).astype(_F32)[:, None], (t % GRID_W).astype(_F32)[:, None])
    return _rotary_tables(pos * lane_f[None, :], first, second)


def _layer(x2d, batch, seq, tabs, norm_pre, w_in, gq, gk, bd, bias, gains, w_out, npost):
    tab_a, tab_b = tabs
    ga, gb, gc = gains
    a1, a4, a16, cqkv, bq, bkv, az, bz, cz = _inproj(x2d, seq, norm_pre, w_in, tab_a, tab_b, gq, gk, bd)
    radii = {dil: window // (2 * dil) for window, dil in A_PATTERNS}
    o4, l4 = _attn_a_dil(a4, batch, seq, 4, radii[4])
    o16, l16 = _attn_a_dil(a16, batch, seq, 16, radii[16])
    ya = _attn_a_final(a1, o4, l4, o16, l16, az, ga, batch, seq, radii[1])
    yb = _attn_b(bq, bkv, bz, gb, batch, seq)
    yc = _attn_c(cqkv, bias, cz, gc, batch, seq)
    return _outproj(ya, yb, yc, w_out, npost, x2d)


def kernel(x_prompt, x_sample, norm_pre, w_in, q_norm, k_norm, rel_bias, branch_gain, w_out, norm_post):
    depth = w_in.shape[0]
    w_in_p = w_in.astype(_BF16)
    w_out_p = w_out.astype(_BF16)
    bd = jnp.asarray(np.kron(np.eye(B_W // HEAD_DIM), np.full((HEAD_DIM, HEAD_DIM), 1.0 / HEAD_DIM)), _BF16)
    max_seq = max(x_prompt.shape[1], x_sample.shape[1])
    tabs = (_rope_tables(max_seq), _axial_tables(max_seq))
    groups = []
    for x in (x_prompt, x_sample):
        batch, seq, _ = x.shape
        groups.append(dict(y=x.reshape(batch * seq, D_MODEL), batch=batch, seq=seq, tabs=tabs))
    for l in range(depth):
        bias = _bias_table(rel_bias[l])
        gq = jnp.tile(q_norm[l][None, :], (1, LANES // HEAD_DIM))
        gk = jnp.tile(k_norm[l][None, :], (1, LANES // HEAD_DIM))
        gain = branch_gain[l][None, :]
        gains = (gain[:, 0:A_W], gain[:, A_W:A_W + B_W], gain[:, A_W + B_W:MIX_W])
        for g in groups:
            g["y"] = _layer(g["y"], g["batch"], g["seq"], g["tabs"], norm_pre[l][None, :], w_in_p[l],
                            gq, gk, bd, bias, gains, w_out_p[l], norm_post[l][None, :])
    return tuple(g["y"].reshape(g["batch"], g["seq"], D_MODEL) for g in groups)
```

```python
import functools
import math

import jax
import jax.numpy as jnp
import numpy as np
from jax import lax
from jax.experimental import pallas as pl
from jax.experimental.pallas import tpu as pltpu

D_MODEL = 1024
HEAD_DIM = 64
GRID_W = 64
EPS = 1e-6
A_HEADS = 6
A_PATTERNS = ((128, 1), (512, 4), (2048, 16))
ROPE_THETA = 500000.0
ROPE_DIMS = HEAD_DIM // 4
B_HEADS = 4
B_KV_HEADS = 2
AXIAL_THETA = 10000.0
C_HEADS = 6
NA_ROWS = 8
NA_COLS = 16

A_W = A_HEADS * HEAD_DIM
B_W = B_HEADS * HEAD_DIM
B_KV_W = B_KV_HEADS * HEAD_DIM
C_W = C_HEADS * HEAD_DIM
MIX_W = A_W + B_W + C_W
QKV_W = 3 * A_W
OFF_A = 0
OFF_AZ = OFF_A + QKV_W
OFF_B = OFF_AZ + A_W
OFF_BZ = OFF_B + B_W + 2 * B_KV_W
OFF_C = OFF_BZ + B_W
OFF_CZ = OFF_C + QKV_W

LANES = 128
A_TILES = A_W // LANES
Q_SCALE = HEAD_DIM ** -0.5
LOG2E = math.log2(math.e)
VMEM_LIMIT = 56 * 1024 * 1024

IN_ROW_TILE = 512
OUT_ROW_TILE = 1024
A_SUB = 128
A_STEP_TOKENS = 2048
A1_Q_BLOCK = 1024
B_TQ = 512
B_TK = 2048
B_KV_UNROLL = 4
C_ROWS = 4
C_WIN = 12
C_TQ = C_ROWS * GRID_W
C_STEP_TILES = 8
C_NK = C_WIN * GRID_W
C_KV_PREFETCH_BYTES = 2 * 1024 * 1024
NEG_INF = float("-inf")

_F32 = jnp.float32
_BF16 = jnp.bfloat16


def _cparams(n_grid):
    return pltpu.CompilerParams(dimension_semantics=("arbitrary",) * n_grid,
                                vmem_limit_bytes=VMEM_LIMIT)


def _lane_lo(shape):
    return lax.broadcasted_iota(jnp.int32, shape, len(shape) - 1) % LANES < HEAD_DIM


def _stack_heads(q):
    lo = _lane_lo(q.shape)
    zero = jnp.zeros_like(q)
    return jnp.concatenate([jnp.where(lo, q, zero), jnp.where(lo, zero, q)], axis=0)


def _dot_nt(a, b):
    return lax.dot_general(a, b, (((1,), (1,)), ((), ())), preferred_element_type=_F32)


def _gated_norm(y, z, gain):
    hz = 0.5 * z.astype(_F32)
    u = y * (hz + hz * jnp.tanh(hz))
    r = lax.rsqrt(jnp.mean(u * u, axis=-1, keepdims=True) + EPS)
    return u * r * gain


def _rotate_tile(t, tab_ref, shift):
    up = pltpu.roll(t, LANES - shift, axis=1)
    dn = pltpu.roll(t, shift, axis=1)
    return t * tab_ref[0] + up * tab_ref[1] + dn * tab_ref[2]


def _inproj_kernel(x_ref, g_ref, w_ref, ta_ref, tb_ref, gq_ref, gk_ref, bd_ref,
                   a1_ref, a4_ref, a16_ref, c_ref, bq_ref, bkv_ref, az_ref, bz_ref, cz_ref, scr_ref):
    x = x_ref[...]
    tm = x.shape[0]
    r = lax.rsqrt(jnp.mean(x * x, axis=-1, keepdims=True) + EPS)
    h = (x * r * g_ref[...]).astype(_BF16)
    p = jnp.dot(h, w_ref[...], preferred_element_type=_F32)

    for j in range(QKV_W // LANES):
        t = p[:, OFF_A + j * LANES:OFF_A + (j + 1) * LANES]
        if j < 2 * A_TILES:
            t = _rotate_tile(t, ta_ref, ROPE_DIMS // 2)
        if j < A_TILES:
            t = t * (Q_SCALE * LOG2E)
        a1_ref[:, j * LANES:(j + 1) * LANES] = t.astype(_BF16)
        scr_ref[j] = t
    for dil, ref in ((4, a4_ref), (16, a16_ref)):
        for res in range(dil):
            for j in range(QKV_W // LANES):
                rows = scr_ref[j, pl.ds(res, tm // dil, stride=dil), :]
                ref[res, :, j * LANES:(j + 1) * LANES] = rows.astype(_BF16)

    c_ref[:, 0:C_W] = (p[:, OFF_C:OFF_C + C_W] * (Q_SCALE * LOG2E)).astype(_BF16)
    c_ref[:, C_W:QKV_W] = p[:, OFF_C + C_W:OFF_C + QKV_W].astype(_BF16)

    bd = bd_ref[...]

    def headnorm(t, gain):
        width = t.shape[1]
        sq = t * t
        hi = sq.astype(_BF16)
        lo = (sq - hi.astype(_F32)).astype(_BF16)
        ms = (jnp.dot(hi, bd[:width, :width], preferred_element_type=_F32)
              + jnp.dot(lo, bd[:width, :width], preferred_element_type=_F32))
        return t * lax.rsqrt(ms + EPS) * gain

    qn = headnorm(p[:, OFF_B:OFF_B + B_W], jnp.concatenate([gq_ref[...]] * (B_W // LANES), axis=1))
    for j in range(B_W // LANES):
        t = _rotate_tile(qn[:, j * LANES:(j + 1) * LANES], tb_ref, HEAD_DIM // 4) * (Q_SCALE * LOG2E)
        bq_ref[:, j * LANES:(j + 1) * LANES] = t.astype(_BF16)
    lo_half = _lane_lo((tm, LANES))
    kn = _rotate_tile(headnorm(p[:, OFF_B + B_W:OFF_B + B_W + B_KV_W], gk_ref[...]), tb_ref, HEAD_DIM // 4)
    k_sw = pltpu.roll(kn, HEAD_DIM, axis=1)
    bkv_ref[:, 0:LANES] = jnp.where(lo_half, kn, k_sw).astype(_BF16)
    bkv_ref[:, LANES:2 * LANES] = jnp.where(lo_half, k_sw, kn).astype(_BF16)
    vv = p[:, OFF_B + B_W + B_KV_W:OFF_B + B_W + 2 * B_KV_W]
    bkv_ref[:, B_W:B_W + LANES] = jnp.where(lo_half, vv, 1.0).astype(_BF16)
    bkv_ref[:, B_W + LANES:2 * B_W] = jnp.where(lo_half, pltpu.roll(vv, HEAD_DIM, axis=1), 1.0).astype(_BF16)

    az_ref[...] = p[:, OFF_AZ:OFF_AZ + A_W].astype(_BF16)
    bz_ref[...] = p[:, OFF_BZ:OFF_BZ + B_W].astype(_BF16)
    cz_ref[...] = p[:, OFF_CZ:OFF_CZ + C_W].astype(_BF16)


def _inproj(x2d, seq, norm_pre, w, tab_a, tab_b, gq, gk, bd):
    T = x2d.shape[0]
    tm = min(IN_ROW_TILE, seq)
    n_seq = seq // tm
    nw = w.shape[1]
    const = lambda i: (0, 0)
    row = lambda i: (i, 0)
    row3 = lambda i: (0, i, 0)
    tab = lambda i: (0, i % n_seq, 0)
    out_shape = (
        jax.ShapeDtypeStruct((T, QKV_W), _BF16),
        jax.ShapeDtypeStruct((4, T // 4, QKV_W), _BF16),
        jax.ShapeDtypeStruct((16, T // 16, QKV_W), _BF16),
        jax.ShapeDtypeStruct((T, QKV_W), _BF16),
        jax.ShapeDtypeStruct((T, B_W), _BF16),
        jax.ShapeDtypeStruct((T, 2 * B_W), _BF16),
        jax.ShapeDtypeStruct((T, A_W), _BF16),
        jax.ShapeDtypeStruct((T, B_W), _BF16),
        jax.ShapeDtypeStruct((T, C_W), _BF16),
    )
    out_specs = (
        pl.BlockSpec((tm, QKV_W), row),
        pl.BlockSpec((4, tm // 4, QKV_W), row3),
        pl.BlockSpec((16, tm // 16, QKV_W), row3),
        pl.BlockSpec((tm, QKV_W), row),
        pl.BlockSpec((tm, B_W), row),
        pl.BlockSpec((tm, 2 * B_W), row),
        pl.BlockSpec((tm, A_W), row),
        pl.BlockSpec((tm, B_W), row),
        pl.BlockSpec((tm, C_W), row),
    )
    in_specs = [
        pl.BlockSpec((tm, D_MODEL), row),
        pl.BlockSpec((1, D_MODEL), const),
        pl.BlockSpec((D_MODEL, nw), const, pipeline_mode=pl.Buffered(1)),
        pl.BlockSpec((3, tm, LANES), tab),
        pl.BlockSpec((3, tm, LANES), tab),
        pl.BlockSpec((1, LANES), const),
        pl.BlockSpec((1, LANES), const),
        pl.BlockSpec((B_W, B_W), const),
    ]
    return pl.pallas_call(
        _inproj_kernel,
        grid=(T // tm,),
        in_specs=in_specs,
        out_specs=out_specs,
        out_shape=out_shape,
        scratch_shapes=[pltpu.VMEM((QKV_W // LANES, tm, LANES), _F32)],
        compiler_params=_cparams(1),
        name="inproj",
    )(x2d, norm_pre, w, tab_a, tab_b, gq, gk, bd)


def _softmax_pv(q2, kp, vp, bias):
    s = _dot_nt(q2, kp) + bias
    m = jnp.max(s, axis=1, keepdims=True)
    e = jnp.exp2(s - m).astype(_BF16)
    return jnp.dot(e, jnp.concatenate([vp, jnp.ones_like(vp)], axis=1), preferred_element_type=_F32), m


def _unstack_heads(ol, m):
    n = ol.shape[0] // 2
    lo = _lane_lo((n, LANES))
    num = jnp.where(lo, ol[:n, :LANES], ol[n:, :LANES])
    den = jnp.where(lo, ol[:n, LANES:], ol[n:, LANES:])
    return num, den, jnp.where(lo, m[:n], m[n:])


def _finish_heads(ol, m):
    num, den, mx = _unstack_heads(ol, m)
    return num / den, mx + jnp.log2(den)


def _band_tile(q, k_ref, v_ref, kidx, kstart, nk, cols, bias):
    kp = k_ref[kidx + (pl.ds(kstart, nk), cols)]
    vp = v_ref[kidx + (pl.ds(kstart, nk), cols)]
    return _finish_heads(*_softmax_pv(_stack_heads(q), kp, vp, bias))


def _band_delta(sub, nk):
    row = lax.broadcasted_iota(jnp.int32, (2 * sub, nk), 0) % sub
    col = lax.broadcasted_iota(jnp.int32, (2 * sub, nk), 1)
    return col - row


def _band_bias(delta, shift, radius):
    rel = delta + shift
    return jnp.where((rel >= -radius) & (rel <= radius), 0.0, NEG_INF).astype(_F32)


def _attn_a_dil_kernel(q_ref, k_ref, v_ref, o_ref, lse_ref, *, dil, q_block, seq_len, radius, sub):
    nk = min(sub + 2 * radius, seq_len)
    base = pl.program_id(1) * q_block
    delta = _band_delta(sub, nk)
    windows = []
    for si in range(q_block // sub):
        l0 = base + si * sub
        kstart = pl.multiple_of(jnp.clip(l0 - radius, 0, seq_len - nk), HEAD_DIM)
        windows.append((kstart, _band_bias(delta, kstart - l0, radius)))

    def body(res, carry):
        for si, (kstart, bias) in enumerate(windows):
            for p in range(A_TILES):
                cols = slice(p * LANES, (p + 1) * LANES)
                q = q_ref[res, si * sub:(si + 1) * sub, cols]
                o, lse = _band_tile(q, k_ref, v_ref, (res,), kstart, nk, cols, bias)
                rows = pl.ds(si * sub * dil + res, sub, stride=dil)
                o_ref[p, rows, :] = o
                lse_ref[p, rows, :] = lse
        return carry

    def pair_body(rp, carry):
        for p in range(A_TILES):
            cols = slice(p * LANES, (p + 1) * LANES)
            res = (2 * rp, 2 * rp + 1)
            q2 = jnp.concatenate([_stack_heads(q_ref[r, :, cols]) for r in res], axis=0)
            kp = jnp.concatenate([k_ref[r, :, cols] for r in res], axis=0)
            vp = jnp.concatenate([v_ref[r, :, cols] for r in res], axis=0)
            ol, m = _softmax_pv(q2, kp, vp, pair_bias)
            for i, r in enumerate(res):
                blk = slice(i * 2 * sub, (i + 1) * 2 * sub)
                rows = pl.ds(r, sub, stride=dil)
                o_ref[p, rows, :], lse_ref[p, rows, :] = _finish_heads(ol[blk], m[blk])
        return carry

    if seq_len == q_block == sub and 2 * nk <= 2 * LANES and dil % 2 == 0:
        bias = windows[0][1]
        neg = jnp.full_like(bias, NEG_INF)
        pair_bias = jnp.concatenate([jnp.concatenate([bias, neg], axis=1),
                                     jnp.concatenate([neg, bias], axis=1)], axis=0)
        lax.fori_loop(0, dil // 2, pair_body, 0, unroll=min(dil // 2, 4))
    else:
        lax.fori_loop(0, dil, body, 0, unroll=max(1, min(dil, 8 * sub // q_block)))


def _attn_a_dil(a_res, batch, seq, dil, radius):
    L = seq // dil
    tl = min(max(A_STEP_TOKENS // dil, A_SUB), L)
    sub = min(A_SUB, tl)
    n_lt = L // tl
    T = batch * seq
    kern = functools.partial(_attn_a_dil_kernel, dil=dil, q_block=tl, seq_len=L, radius=radius, sub=sub)
    q_spec = pl.BlockSpec((dil, tl, A_W), lambda b, t: (0, b * n_lt + t, 0))
    k_spec = pl.BlockSpec((dil, L, A_W), lambda b, t: (0, b, 1))
    v_spec = pl.BlockSpec((dil, L, A_W), lambda b, t: (0, b, 2))
    o_spec = pl.BlockSpec((A_TILES, tl * dil, LANES), lambda b, t: (0, b * n_lt + t, 0))
    out_shape = (jax.ShapeDtypeStruct((A_TILES, T, LANES), _F32),) * 2
    return pl.pallas_call(
        kern,
        grid=(batch, n_lt),
        in_specs=[q_spec, k_spec, v_spec],
        out_specs=(o_spec, o_spec),
        out_shape=out_shape,
        compiler_params=_cparams(2),
        name=f"attn_a_d{dil}",
    )(a_res, a_res, a_res)


def _attn_a_final_kernel(q_ref, k_ref, v_ref, o4_ref, l4_ref, o16_ref, l16_ref, z_ref, g_ref, y_ref,
                         *, q_block, seq_len, radius, sub):
    nk = min(sub + 2 * radius, seq_len)
    n_sub = q_block // sub
    base = pl.program_id(1) * q_block
    delta = _band_delta(sub, nk)
    inner_bias = _band_bias(delta, -radius, radius)
    for si in range(n_sub):
        l0 = base + si * sub
        kstart = pl.multiple_of(jnp.clip(l0 - radius, 0, seq_len - nk), HEAD_DIM)
        inner = nk == sub + 2 * radius and 0 < si < n_sub - 1
        bias = inner_bias if inner else _band_bias(delta, kstart - l0, radius)
        rows = slice(si * sub, (si + 1) * sub)
        tiles = []
        for p in range(A_TILES):
            cols = slice(p * LANES, (p + 1) * LANES)
            kp = k_ref[pl.ds(kstart, nk), cols]
            vp = v_ref[pl.ds(kstart, nk), cols]
            num1, den1, m1 = _unstack_heads(*_softmax_pv(_stack_heads(q_ref[rows, cols]), kp, vp, bias))
            l4, l16 = l4_ref[p, rows, :], l16_ref[p, rows, :]
            m = jnp.maximum(jnp.maximum(m1, l4), l16)
            w1, e4, e16 = jnp.exp2(m1 - m), jnp.exp2(l4 - m), jnp.exp2(l16 - m)
            mix = w1 * num1 + e4 * o4_ref[p, rows, :] + e16 * o16_ref[p, rows, :]
            tiles.append(mix / (w1 * den1 + e4 + e16))
        ya = jnp.concatenate(tiles, axis=1)
        y_ref[rows, :] = _gated_norm(ya, z_ref[rows, :], g_ref[...]).astype(_BF16)


def _attn_a_final(a1, o4, l4, o16, l16, az, gain, batch, seq, radius):
    T = a1.shape[0]
    tl = min(A1_Q_BLOCK, seq)
    sub = min(A_SUB, tl)
    n_lt = seq // tl
    kern = functools.partial(_attn_a_final_kernel, q_block=tl, seq_len=seq, radius=radius, sub=sub)
    tok = lambda b, t: (b * n_lt + t, 0)
    tok3 = lambda b, t: (0, b * n_lt + t, 0)
    st_spec = pl.BlockSpec((A_TILES, tl, LANES), tok3)
    return pl.pallas_call(
        kern,
        grid=(batch, n_lt),
        in_specs=[
            pl.BlockSpec((tl, A_W), tok),
            pl.BlockSpec((seq, A_W), lambda b, t: (b, 1)),
            pl.BlockSpec((seq, A_W), lambda b, t: (b, 2)),
            st_spec, st_spec, st_spec, st_spec,
            pl.BlockSpec((tl, A_W), tok),
            pl.BlockSpec((1, A_W), lambda b, t: (0, 0)),
        ],
        out_specs=pl.BlockSpec((tl, A_W), tok),
        out_shape=jax.ShapeDtypeStruct((T, A_W), _BF16),
        compiler_params=_cparams(2),
        name="attn_a_d1",
    )(a1, a1, a1, o4, l4, o16, l16, az, gain)


def _attn_b_kernel(q_ref, kv_ref, z_ref, g_ref, y_ref, *, seq, tk, tq):
    n_grp = B_W // LANES
    n_kv = seq // tk
    for h in range(q_ref.shape[0] // tq):
        rows = slice(h * tq, (h + 1) * tq)
        q2s = [_stack_heads(q_ref[rows, g * LANES:(g + 1) * LANES]) for g in range(n_grp)]

        def body(kt, carry):
            ks = pl.multiple_of(kt * tk, tk)
            out = []
            for g in range(n_grp):
                m, acc = carry[g]
                k = kv_ref[pl.ds(ks, tk), g * LANES:(g + 1) * LANES]
                v = kv_ref[pl.ds(ks, tk), B_W + g * LANES:B_W + (g + 1) * LANES]
                s = _dot_nt(q2s[g], k)
                m_new = jnp.maximum(m, jnp.max(s, axis=1, keepdims=True))
                alpha = jnp.exp2(m - m_new)
                p = jnp.exp2(s - m_new)
                acc = alpha * acc + jnp.dot(p.astype(_BF16), v, preferred_element_type=_F32)
                out.append((m_new, acc))
            return tuple(out)

        init = tuple((jnp.full((2 * tq, 1), NEG_INF, _F32), jnp.zeros((2 * tq, LANES), _F32))
                     for _ in range(n_grp))
        final = (body(0, init) if n_kv == 1
                 else lax.fori_loop(0, n_kv, body, init, unroll=min(n_kv, B_KV_UNROLL)))
        tiles = []
        for g in range(n_grp):
            acc = final[g][1]
            head0, head1 = acc[:tq], acc[tq:]
            tiles.append(jnp.where(_lane_lo((tq, LANES)), head0 / pltpu.roll(head0, HEAD_DIM, axis=1),
                                   pltpu.roll(head1, HEAD_DIM, axis=1) / head1))
        yb = jnp.concatenate(tiles, axis=1)
        y_ref[rows, :] = _gated_norm(yb, z_ref[rows, :], g_ref[...]).astype(_BF16)


def _attn_b(bq, bkv, bz, gain, batch, seq):
    T = bq.shape[0]
    tq = min(B_TQ, seq)
    tk = min(B_TK, seq)
    step_rows = tq * (2 if seq == tk and seq % (2 * tq) == 0 else 1)
    nq = seq // step_rows
    kern = functools.partial(_attn_b_kernel, seq=seq, tk=tk, tq=tq)
    tok = lambda b, t: (b * nq + t, 0)
    return pl.pallas_call(
        kern,
        grid=(batch, nq),
        in_specs=[
            pl.BlockSpec((step_rows, B_W), tok),
            pl.BlockSpec((seq, 2 * B_W), lambda b, t: (b, 0)),
            pl.BlockSpec((step_rows, B_W), tok),
            pl.BlockSpec((1, B_W), lambda b, t: (0, 0)),
        ],
        out_specs=pl.BlockSpec((step_rows, B_W), tok),
        out_shape=jax.ShapeDtypeStruct((T, B_W), _BF16),
        compiler_params=_cparams(2),
        name="attn_b",
    )(bq, bkv, bz, gain)


def _c_tile_types():
    half = NA_ROWS // 2
    first = dict(ro_base=NA_ROWS - 1, start=[0] * C_ROWS)
    inner = dict(ro_base=NA_ROWS - 1 - half, start=list(range(C_ROWS)))
    last = dict(ro_base=NA_ROWS - 1 - (C_WIN - C_ROWS), start=[C_WIN - NA_ROWS] * C_ROWS)
    return (first, inner, last)


def _bias_kernel(rb_ref, out_ref):
    h = pl.program_id(0)
    n_ro = 2 * NA_ROWS - 1
    n_co = 2 * NA_COLS - 1
    qi = lax.broadcasted_iota(jnp.int32, (GRID_W, LANES), 0)
    lane = lax.broadcasted_iota(jnp.int32, (GRID_W, LANES), 1)
    kc = lane % GRID_W
    d = kc - qi
    cs = jnp.clip(qi - NA_COLS // 2, 0, GRID_W - NA_COLS)
    col_valid = (kc >= cs) & (kc < cs + NA_COLS)
    hits = [d == co - (NA_COLS - 1) for co in range(n_co)]
    neg = jnp.full((GRID_W, LANES), NEG_INF, _F32)
    rows = []
    for ro in range(n_ro):
        e = neg
        for co in range(n_co):
            e = jnp.where(hits[co], rb_ref[(h * n_ro + ro) * n_co + co] * LOG2E, e)
        rows.append(jnp.where(col_valid, e, neg))
    lo = lane < GRID_W
    for t, spec in enumerate(_c_tile_types()):
        for j in range(C_ROWS):
            def blk(kk):
                ok = spec["start"][j] <= kk < spec["start"][j] + NA_ROWS
                return rows[kk - j + spec["ro_base"]] if ok else neg
            tiles = [jnp.where(lo, blk(2 * i), blk(2 * i + 1)) for i in range(C_WIN // 2)]
            out_ref[t, j * GRID_W:(j + 1) * GRID_W, :] = jnp.concatenate(tiles, axis=1)


def _bias_table(rel_bias):
    n_heads = rel_bias.shape[0] * C_HEADS
    flat = rel_bias.reshape(-1)
    return pl.pallas_call(
        _bias_kernel,
        grid=(n_heads,),
        in_specs=[pl.BlockSpec(memory_space=pltpu.SMEM)],
        out_specs=pl.BlockSpec((3, None, C_TQ, C_NK), lambda h: (0, h, 0, 0)),
        out_shape=jax.ShapeDtypeStruct((3, n_heads, C_TQ, C_NK), _F32),
        compiler_params=_cparams(1),
        name="c_bias",
    )(flat)


def _attn_c_kernel(q_ref, k_ref, v_ref, b_ref, z_ref, g_ref, y_ref, *, grid_rows):
    n_tiles = grid_rows // C_ROWS
    for h in range(C_STEP_TILES):
        tile = pl.program_id(1) * C_STEP_TILES + h
        kind = jnp.where(tile == 0, 0, jnp.where(tile == n_tiles - 1, 2, 1))
        r0 = tile * C_ROWS
        ws = jnp.clip(r0 - NA_ROWS // 2, 0, grid_rows - C_WIN)
        ks = pl.multiple_of(ws * GRID_W, GRID_W)
        rows = slice(h * C_TQ, (h + 1) * C_TQ)
        tiles = []
        for p in range(C_W // LANES):
            cols = slice(p * LANES, (p + 1) * LANES)
            num, den, _ = _unstack_heads(*_softmax_pv(_stack_heads(q_ref[rows, cols]), k_ref[pl.ds(ks, C_NK), cols],
                                                      v_ref[pl.ds(ks, C_NK), cols], b_ref[kind, p]))
            tiles.append(num / den)
        yc = jnp.concatenate(tiles, axis=1)
        y_ref[rows, :] = _gated_norm(yc, z_ref[rows, :], g_ref[...]).astype(_BF16)


def _attn_c(cqkv, bias, cz, gain, batch, seq):
    T = cqkv.shape[0]
    grid_rows = seq // GRID_W
    n_tiles = seq // C_TQ
    assert grid_rows >= C_WIN and n_tiles % C_STEP_TILES == 0
    nq = n_tiles // C_STEP_TILES
    tq = C_STEP_TILES * C_TQ
    n_pair = C_W // LANES
    bias, layer = bias
    bias = bias.reshape(3, -1, 2 * C_TQ, C_NK)

    bias_spec = pl.BlockSpec((3, n_pair, 2 * C_TQ, C_NK), lambda b, t: (0, layer, 0, 0),
                             pipeline_mode=pl.Buffered(1))
    kern = functools.partial(_attn_c_kernel, grid_rows=grid_rows)
    tok = lambda b, t: (b * nq + t, 0)
    kv_buffers = 2 if seq * C_W * 2 <= C_KV_PREFETCH_BYTES else 1
    resident = lambda col: pl.BlockSpec((seq, C_W), lambda b, t: (b, col), pipeline_mode=pl.Buffered(kv_buffers))
    return pl.pallas_call(
        kern,
        grid=(batch, nq),
        in_specs=[pl.BlockSpec((tq, C_W), tok), resident(1), resident(2), bias_spec]
        + [pl.BlockSpec((tq, C_W), tok), pl.BlockSpec((1, C_W), lambda b, t: (0, 0))],
        out_specs=pl.BlockSpec((tq, C_W), tok),
        out_shape=jax.ShapeDtypeStruct((T, C_W), _BF16),
        compiler_params=_cparams(2),
        name="attn_c",
    )(cqkv, cqkv, cqkv, bias, cz, gain)


def _outproj_kernel(ya_ref, yb_ref, yc_ref, w_ref, npost_ref, x_ref, out_ref):
    y = jnp.concatenate([ya_ref[...], yb_ref[...], yc_ref[...]], axis=1)
    t = jnp.dot(y, w_ref[...], preferred_element_type=_F32)
    r = lax.rsqrt(jnp.mean(t * t, axis=-1, keepdims=True) + EPS)
    out_ref[...] = x_ref[...] + t * r * npost_ref[...]


def _outproj(ya, yb, yc, w, npost, x2d):
    T = x2d.shape[0]
    tm = min(OUT_ROW_TILE, T)
    row = lambda i: (i, 0)
    const = lambda i: (0, 0)
    return pl.pallas_call(
        _outproj_kernel,
        grid=(T // tm,),
        in_specs=[pl.BlockSpec((tm, A_W), row), pl.BlockSpec((tm, B_W), row), pl.BlockSpec((tm, C_W), row),
                  pl.BlockSpec((MIX_W, D_MODEL), const, pipeline_mode=pl.Buffered(1)),
                  pl.BlockSpec((1, D_MODEL), const), pl.BlockSpec((tm, D_MODEL), row)],
        out_specs=pl.BlockSpec((tm, D_MODEL), row),
        out_shape=jax.ShapeDtypeStruct((T, D_MODEL), _F32),
        compiler_params=_cparams(1),
        name="outproj",
    )(ya, yb, yc, w, npost, x2d)


def _lane_freqs(freqs, n_rot, width):
    lane = np.arange(LANES) % width
    idx = lane % (n_rot // 2)
    rot = lane < n_rot
    first = rot & (lane % n_rot < n_rot // 2)
    second = rot & ~first
    return jnp.where(rot, freqs[idx], 0.0), first, second


def _rotary_tables(ang, first, second):
    cos, sin = jnp.cos(ang), jnp.sin(ang)
    zero = jnp.zeros_like(sin)
    return jnp.stack([cos, jnp.where(first, -sin, zero), jnp.where(second, sin, zero)])


def _rope_tables(seq):
    freqs = ROPE_THETA ** (-jnp.arange(0, ROPE_DIMS, 2, dtype=_F32) / ROPE_DIMS)
    lane_f, first, second = _lane_freqs(freqs, ROPE_DIMS, HEAD_DIM)
    ang = jnp.arange(seq, dtype=_F32)[:, None] * lane_f[None, :]
    return _rotary_tables(ang, first, second)


def _axial_tables(seq):
    half = HEAD_DIM // 2
    freqs = AXIAL_THETA ** (-jnp.arange(0, half, 2, dtype=_F32) / half)
    lane_f, first, second = _lane_freqs(freqs, half, half)
    t = jnp.arange(seq)
    by_row = (np.arange(LANES) % HEAD_DIM) < half
    pos = jnp.where(by_row[None, :], (t // GRID_W).astype(_F32)[:, None], (t % GRID_W).astype(_F32)[:, None])
    return _rotary_tables(pos * lane_f[None, :], first, second)


def _layer(x2d, batch, seq, tabs, norm_pre, w_in, gq, gk, bd, bias, gains, w_out, npost):
    tab_a, tab_b = tabs
    ga, gb, gc = gains
    a1, a4, a16, cqkv, bq, bkv, az, bz, cz = _inproj(x2d, seq, norm_pre, w_in, tab_a, tab_b, gq, gk, bd)
    radii = {dil: window // (2 * dil) for window, dil in A_PATTERNS}
    o4, l4 = _attn_a_dil(a4, batch, seq, 4, radii[4])
    o16, l16 = _attn_a_dil(a16, batch, seq, 16, radii[16])
    ya = _attn_a_final(a1, o4, l4, o16, l16, az, ga, batch, seq, radii[1])
    yb = _attn_b(bq, bkv, bz, gb, batch, seq)
    yc = _attn_c(cqkv, bias, cz, gc, batch, seq)
    return _outproj(ya, yb, yc, w_out, npost, x2d)


def kernel(x_prompt, x_sample, norm_pre, w_in, q_norm, k_norm, rel_bias, branch_gain, w_out, norm_post):
    depth = w_in.shape[0]
    w_in_p = w_in.astype(_BF16)
    w_out_p = w_out.astype(_BF16)
    bd = jnp.asarray(np.kron(np.eye(B_W // HEAD_DIM), np.full((HEAD_DIM, HEAD_DIM), 1.0 / HEAD_DIM)), _BF16)
    max_seq = max(x_prompt.shape[1], x_sample.shape[1])
    tabs = (_rope_tables(max_seq), _axial_tables(max_seq))
    groups = []
    for x in (x_prompt, x_sample):
        batch, seq, _ = x.shape
        groups.append(dict(y=x.reshape(batch * seq, D_MODEL), batch=batch, seq=seq, tabs=tabs))
    bias_all = _bias_table(rel_bias)
    for l in range(depth):
        bias = (bias_all, l)
        gq = jnp.tile(q_norm[l][None, :], (1, LANES // HEAD_DIM))
        gk = jnp.tile(k_norm[l][None, :], (1, LANES // HEAD_DIM))
        gain = branch_gain[l][None, :]
        gains = (gain[:, 0:A_W], gain[:, A_W:A_W + B_W], gain[:, A_W + B_W:MIX_W])
        for g in groups:
            g["y"] = _layer(g["y"], g["batch"], g["seq"], g["tabs"], norm_pre[l][None, :], w_in_p[l],
                            gq, gk, bd, bias, gains, w_out_p[l], norm_post[l][None, :])
    return tuple(g["y"].reshape(g["batch"], g["seq"], D_MODEL) for g in groups)
```
